```python
import jax, jax.numpy as jnp
from jax import lax
import numpy as np

D_MODEL = 2048
BATCH = 4
SEQ = 2048
DEPTH = 2

HEAD_DIM = 128
N_Q_HEADS = D_MODEL // HEAD_DIM
N_KV_HEADS = N_Q_HEADS // 4
Q_PER_KV = N_Q_HEADS // N_KV_HEADS
ATTN_DIM = N_Q_HEADS * HEAD_DIM
KV_DIM = N_KV_HEADS * HEAD_DIM
AXIS_DIM = HEAD_DIM // 2
ROPE_THETA = 10000.0
Q_BLOCK = 128
GRID_W = 64
CONV_DIM = D_MODEL // 2
CONV_K = 3
D_FF = ((8 * D_MODEL // 3 + 255) // 256) * 256
PLE_DIM = 256
EPS = 1e-6

SPLIT_SIZES = (ATTN_DIM, KV_DIM, KV_DIM, CONV_DIM, CONV_DIM, CONV_DIM, D_MODEL, D_MODEL)
SPLIT_IDX = tuple(int(v) for v in np.cumsum(SPLIT_SIZES)[:-1])
IN_DIM = int(sum(SPLIT_SIZES))

kernel_name = "hybrid_gqa_axialrope_shortconv_convffn_ple"


def rmsnorm(x, g):
    xf = x.astype(jnp.float32)
    y = xf * lax.rsqrt(jnp.mean(xf * xf, axis=-1, keepdims=True) + EPS)
    return (y * g.astype(jnp.float32)).astype(x.dtype)


def dwconv3(x, w, b):
    c = x.shape[-1]
    y = lax.conv_general_dilated(
        x, w.astype(x.dtype)[:, None, :], window_strides=(1,),
        padding=[((CONV_K - 1) // 2, (CONV_K - 1) // 2)],
        dimension_numbers=("NWC", "WIO", "NWC"), feature_group_count=c)
    return y + b.astype(x.dtype)


def axial_rope_tables(seq):
    rows = seq // GRID_W
    row_ids = jnp.repeat(jnp.arange(rows, dtype=jnp.float32), GRID_W)
    col_ids = jnp.tile(jnp.arange(GRID_W, dtype=jnp.float32), rows)
    inv_freq = 1.0 / (ROPE_THETA ** (jnp.arange(0, AXIS_DIM, 2, dtype=jnp.float32) / AXIS_DIM))
    ang_r = row_ids[:, None] * inv_freq[None, :]
    ang_c = col_ids[:, None] * inv_freq[None, :]
    ang = jnp.concatenate([ang_r, ang_r, ang_c, ang_c], axis=-1)
    return jnp.cos(ang), jnp.sin(ang)


def _rot_half(x):
    x1, x2 = jnp.split(x, 2, axis=-1)
    return jnp.concatenate([-x2, x1], axis=-1)


def apply_axial_rope(x, cos, sin):
    rot = jnp.concatenate([_rot_half(x[..., :AXIS_DIM]), _rot_half(x[..., AXIS_DIM:])], axis=-1)
    c = cos[None, :, None, :].astype(x.dtype)
    s = sin[None, :, None, :].astype(x.dtype)
    return x * c + rot * s


def gqa_blocked(q, k, v):
    b, s, _, d = q.shape
    nb = s // Q_BLOCK
    qb = q.reshape(b, nb, Q_BLOCK, N_KV_HEADS, Q_PER_KV, d).transpose(1, 0, 2, 3, 4, 5)
    scale = 1.0 / float(np.sqrt(d))

    def one_block(q_blk):
        sc = jnp.einsum("bqhgd,bkhd->bhgqk", q_blk, k).astype(jnp.float32) * scale
        pr = jax.nn.softmax(sc, axis=-1).astype(v.dtype)
        return jnp.einsum("bhgqk,bkhd->bqhgd", pr, v)

    out = lax.map(one_block, qb)
    return out.transpose(1, 0, 2, 3, 4, 5).reshape(b, s, N_Q_HEADS * d)


def setup_inputs(seed: int = 0) -> dict:
    key = jax.random.key(seed)
    ks = jax.random.split(key, 24)
    f32 = jnp.float32

    def w(k, shape, fan_in):
        return jax.random.normal(k, shape, f32) * (fan_in ** -0.5)

    def gain(k, shape):
        return 1.0 + 0.02 * jax.random.normal(k, shape, f32)

    def bias(k, shape):
        return 0.02 * jax.random.normal(k, shape, f32)

    return {
        "x": jax.random.normal(ks[0], (BATCH, SEQ, D_MODEL), f32),
        "p": jax.random.normal(ks[1], (DEPTH, BATCH, SEQ, PLE_DIM), f32),
        "norm_mix_g": gain(ks[2], (DEPTH, D_MODEL)),
        "w_in": w(ks[3], (DEPTH, D_MODEL, IN_DIM), D_MODEL),
        "q_norm_g": gain(ks[4], (DEPTH, HEAD_DIM)),
        "k_norm_g": gain(ks[5], (DEPTH, HEAD_DIM)),
        "w_attn_br": w(ks[6], (DEPTH, ATTN_DIM, D_MODEL), ATTN_DIM),
        "conv_mix_w": w(ks[7], (DEPTH, CONV_K, CONV_DIM), CONV_K),
        "conv_mix_b": bias(ks[8], (DEPTH, CONV_DIM)),
        "w_conv_br": w(ks[9], (DEPTH, CONV_DIM, D_MODEL), CONV_DIM),
        "w_mix_out": w(ks[10], (DEPTH, D_MODEL, D_MODEL), D_MODEL),
        "norm_ffn_g": gain(ks[11], (DEPTH, D_MODEL)),
        "w_ffn_in": w(ks[12], (DEPTH, D_MODEL, 2 * D_FF), D_MODEL),
        "conv_ffn_w": w(ks[13], (DEPTH, CONV_K, D_FF), CONV_K),
        "conv_ffn_b": bias(ks[14], (DEPTH, D_FF)),
        "w_ffn_out": w(ks[15], (DEPTH, D_FF, D_MODEL), D_FF),
        "norm_ple_g": gain(ks[16], (DEPTH, D_MODEL)),
        "w_ple_gate": w(ks[17], (DEPTH, D_MODEL, D_MODEL), D_MODEL),
        "w_ple": w(ks[18], (DEPTH, PLE_DIM, D_MODEL), PLE_DIM),
        "final_norm_g": gain(ks[19], (D_MODEL,)),
    }


def reference(x, p, norm_mix_g, w_in, q_norm_g, k_norm_g, w_attn_br, conv_mix_w, conv_mix_b,
              w_conv_br, w_mix_out, norm_ffn_g, w_ffn_in, conv_ffn_w, conv_ffn_b, w_ffn_out,
              norm_ple_g, w_ple_gate, w_ple, final_norm_g):
    b, s, _ = x.shape
    cos, sin = axial_rope_tables(s)
    h = x
    for i in range(DEPTH):
        xn = rmsnorm(h, norm_mix_g[i])
        z = xn @ w_in[i]
        q, k, v, bg, cg, u, ga, gc = jnp.split(z, SPLIT_IDX, axis=-1)
        q = apply_axial_rope(rmsnorm(q.reshape(b, s, N_Q_HEADS, HEAD_DIM), q_norm_g[i]), cos, sin)
        k = apply_axial_rope(rmsnorm(k.reshape(b, s, N_KV_HEADS, HEAD_DIM), k_norm_g[i]), cos, sin)
        v = v.reshape(b, s, N_KV_HEADS, HEAD_DIM)
        attn = gqa_blocked(q, k, v)
        conv = bg * dwconv3(cg * u, conv_mix_w[i], conv_mix_b[i])
        merged = (jax.nn.sigmoid(ga) * (attn @ w_attn_br[i])
                  + jax.nn.sigmoid(gc) * (conv @ w_conv_br[i]))
        h = h + merged @ w_mix_out[i]
        hn = rmsnorm(h, norm_ffn_g[i])
        gate, up = jnp.split(hn @ w_ffn_in[i], 2, axis=-1)
        h = h + (jax.nn.silu(dwconv3(gate, conv_ffn_w[i], conv_ffn_b[i])) * up) @ w_ffn_out[i]
        pn = rmsnorm(h, norm_ple_g[i])
        h = h + jax.nn.sigmoid(pn @ w_ple_gate[i]) * (p[i] @ w_ple[i])
    return rmsnorm(h, final_norm_g)
```

```python
import functools

import numpy as np
import jax
import jax.numpy as jnp
from jax import lax
from jax.experimental import pallas as pl
from jax.experimental.pallas import tpu as pltpu

HEAD_DIM = 128
Q_PER_KV = 4
ROPE_THETA = 10000.0
GRID_W = 64
EPS = 1e-6
CONV_PAD = 8
V7X_VMEM_LIMIT = 56 * 1024 * 1024

F32 = jnp.float32
BF16 = jnp.bfloat16


def _cparams(sem):
    return pltpu.CompilerParams(dimension_semantics=sem, vmem_limit_bytes=V7X_VMEM_LIMIT)


def _rms(x, g):
    ms = jnp.mean(x * x, axis=-1, keepdims=True)
    return x * lax.rsqrt(ms + EPS) * g


def _dot(a, b):
    return jnp.dot(a, b, preferred_element_type=F32)


def _norm_kernel(x_ref, g_ref, o_ref):
    o_ref[...] = _rms(x_ref[...], g_ref[...]).astype(o_ref.dtype)


def _norm_call(x, g, tm=512):
    m, d = x.shape
    return pl.pallas_call(
        _norm_kernel,
        grid=(m // tm,),
        in_specs=[pl.BlockSpec((tm, d), lambda i: (i, 0)),
                  pl.BlockSpec((1, d), lambda i: (0, 0))],
        out_specs=pl.BlockSpec((tm, d), lambda i: (i, 0)),
        out_shape=jax.ShapeDtypeStruct((m, d), BF16),
        compiler_params=_cparams(("parallel",)),
        name="init_norm",
    )(x, g)


def _qkv_kernel(x_ref, w_ref, cos_ref, sa_ref, sb_ref, qg_ref, kg_ref, o_ref, *, rc, n_q_tiles):
    j = pl.program_id(1)
    s = x_ref.shape[1]
    tn = w_ref.shape[1]
    scale = 1.0 / float(np.sqrt(HEAD_DIM))
    gain = jnp.where(j < n_q_tiles, qg_ref[...] * scale, kg_ref[...])
    for c in range(s // rc):
        rows = pl.ds(c * rc, rc)
        z = _dot(x_ref[0, rows, :], w_ref[...])

        @pl.when(j <= n_q_tiles)
        def _():
            cos = cos_ref[rows, :]
            sa = sa_ref[rows, :]
            sb = sb_ref[rows, :]
            for hh in range(tn // HEAD_DIM):
                cols = slice(hh * HEAD_DIM, (hh + 1) * HEAD_DIM)
                y = _rms(z[:, cols], gain)
                y = y * cos + pltpu.roll(y, 96, 1) * sa + pltpu.roll(y, 32, 1) * sb
                o_ref[0, rows, cols] = y.astype(o_ref.dtype)

        @pl.when(j > n_q_tiles)
        def _():
            o_ref[0, rows, :] = z.astype(o_ref.dtype)


def _qkv_call(xn, w_in, cos, sa, sb, qg, kg, *, n_cols, tn=512, rc=512):
    b, s, d = xn.shape
    n_q_tiles = d // tn
    kern = functools.partial(_qkv_kernel, rc=rc, n_q_tiles=n_q_tiles)
    const = lambda bi, j: (0, 0)
    return pl.pallas_call(
        kern,
        grid=(b, n_cols // tn),
        in_specs=[pl.BlockSpec((1, s, d), lambda bi, j: (bi, 0, 0)),
                  pl.BlockSpec((d, tn), lambda bi, j: (0, j)),
                  pl.BlockSpec((s, HEAD_DIM), const),
                  pl.BlockSpec((s, HEAD_DIM), const),
                  pl.BlockSpec((s, HEAD_DIM), const),
                  pl.BlockSpec((1, HEAD_DIM), const),
                  pl.BlockSpec((1, HEAD_DIM), const)],
        out_specs=pl.BlockSpec((1, s, tn), lambda bi, j: (bi, 0, j)),
        out_shape=jax.ShapeDtypeStruct((b, s, n_cols), BF16),
        compiler_params=_cparams(("parallel", "arbitrary")),
        name="qkv_proj",
    )(xn, w_in, cos, sa, sb, qg, kg)


def _dwconv3_rows(pad_ref, c, rc, w):
    base = CONV_PAD + c * rc
    prev = pad_ref[pl.ds(base - 1, rc), :]
    cur = pad_ref[pl.ds(base, rc), :]
    nxt = pad_ref[pl.ds(base + 1, rc), :]
    return prev * w[0:1, :] + cur * w[1:2, :] + nxt * w[2:3, :]


def _zero_halo(pad_ref, s):
    zeros = jnp.zeros((CONV_PAD, pad_ref.shape[1]), pad_ref.dtype)
    pad_ref[pl.ds(0, CONV_PAD), :] = zeros
    pad_ref[pl.ds(CONV_PAD + s, CONV_PAD), :] = zeros


def _convbr_kernel(x_ref, wb_ref, wc_ref, wu_ref, cw_ref, cb_ref, o_ref, pad_ref, *, rc):
    s = x_ref.shape[1]
    _zero_halo(pad_ref, s)
    for c in range(s // rc):
        rows = pl.ds(c * rc, rc)
        x = x_ref[0, rows, :]
        pad_ref[pl.ds(CONV_PAD + c * rc, rc), :] = _dot(x, wc_ref[...]) * _dot(x, wu_ref[...])
    w = cw_ref[...]
    bias = cb_ref[...]
    for c in range(s // rc):
        rows = pl.ds(c * rc, rc)
        conv = _dwconv3_rows(pad_ref, c, rc, w) + bias
        bg = _dot(x_ref[0, rows, :], wb_ref[...])
        o_ref[0, rows, :] = (bg * conv).astype(o_ref.dtype)


def _convbr_call(xn, w_in, conv_w, conv_b, *, col0, width, tn=512, rc=512):
    b, s, d = xn.shape
    t0 = col0 // tn
    nt = width // tn
    kern = functools.partial(_convbr_kernel, rc=rc)
    return pl.pallas_call(
        kern,
        grid=(b, nt),
        in_specs=[pl.BlockSpec((1, s, d), lambda bi, j: (bi, 0, 0)),
                  pl.BlockSpec((d, tn), lambda bi, j: (0, t0 + j)),
                  pl.BlockSpec((d, tn), lambda bi, j: (0, t0 + nt + j)),
                  pl.BlockSpec((d, tn), lambda bi, j: (0, t0 + 2 * nt + j)),
                  pl.BlockSpec((3, tn), lambda bi, j: (0, j)),
                  pl.BlockSpec((1, tn), lambda bi, j: (0, j))],
        out_specs=pl.BlockSpec((1, s, tn), lambda bi, j: (bi, 0, j)),
        out_shape=jax.ShapeDtypeStruct((b, s, width), BF16),
        scratch_shapes=[pltpu.VMEM((s + 2 * CONV_PAD, tn), F32)],
        compiler_params=_cparams(("parallel", "arbitrary")),
        name="conv_branch",
    )(xn, w_in, w_in, w_in, conv_w, conv_b)


def _attn_kernel(q_ref, k_ref, v_ref, o_ref):
    k = k_ref[0]
    v = v_ref[0]
    for g in range(Q_PER_KV):
        cols = slice(g * HEAD_DIM, (g + 1) * HEAD_DIM)
        q = q_ref[0, :, cols]
        sc = lax.dot_general(q, k, (((1,), (1,)), ((), ())), preferred_element_type=F32)
        m = jnp.max(sc, axis=-1, keepdims=True)
        p = jnp.exp(sc - m)
        l = jnp.sum(p, axis=-1, keepdims=True)
        o = _dot(p.astype(BF16), v)
        o_ref[0, :, cols] = (o / l).astype(o_ref.dtype)


def _attn_call(qkv, *, d_model, n_kv, tq=256):
    b, s, _ = qkv.shape
    gw = Q_PER_KV * HEAD_DIM
    k0 = d_model // HEAD_DIM
    v0 = k0 + n_kv
    return pl.pallas_call(
        _attn_kernel,
        grid=(b, n_kv, s // tq),
        in_specs=[pl.BlockSpec((1, tq, gw), lambda bi, h, i: (bi, i, h)),
                  pl.BlockSpec((1, s, HEAD_DIM), lambda bi, h, i: (bi, 0, k0 + h)),
                  pl.BlockSpec((1, s, HEAD_DIM), lambda bi, h, i: (bi, 0, v0 + h))],
        out_specs=pl.BlockSpec((1, tq, gw), lambda bi, h, i: (bi, i, h)),
        out_shape=jax.ShapeDtypeStruct((b, s, d_model), BF16),
        compiler_params=_cparams(("parallel", "parallel", "arbitrary")),
        name="gqa_attention",
    )(qkv, qkv, qkv)


def _merge_kernel(x_ref, a_ref, c_ref, wga_ref, wgc_ref, wa_ref, wc_ref, o_ref):
    x = x_ref[...]
    ga = jax.nn.sigmoid(_dot(x, wga_ref[...]))
    gc = jax.nn.sigmoid(_dot(x, wgc_ref[...]))
    merged = ga * _dot(a_ref[...], wa_ref[...]) + gc * _dot(c_ref[...], wc_ref[...])
    o_ref[...] = merged.astype(o_ref.dtype)


def _merge_call(xn, attn, conv, w_in, w_attn_br, w_conv_br, *, ga_col0, tm=512, tn=512):
    m, d = xn.shape
    cdim = conv.shape[1]
    ga0 = ga_col0 // tn
    gc0 = ga0 + d // tn
    return pl.pallas_call(
        _merge_kernel,
        grid=(m // tm, d // tn),
        in_specs=[pl.BlockSpec((tm, d), lambda i, j: (i, 0)),
                  pl.BlockSpec((tm, d), lambda i, j: (i, 0)),
                  pl.BlockSpec((tm, cdim), lambda i, j: (i, 0)),
                  pl.BlockSpec((d, tn), lambda i, j: (0, ga0 + j)),
                  pl.BlockSpec((d, tn), lambda i, j: (0, gc0 + j)),
                  pl.BlockSpec((d, tn), lambda i, j: (0, j)),
                  pl.BlockSpec((cdim, tn), lambda i, j: (0, j))],
        out_specs=pl.BlockSpec((tm, tn), lambda i, j: (i, j)),
        out_shape=jax.ShapeDtypeStruct((m, d), BF16),
        compiler_params=_cparams(("parallel", "arbitrary")),
        name="gate_merge",
    )(xn, attn, conv, w_in, w_in, w_attn_br, w_conv_br)


def _resid_norm_kernel(a_ref, w_ref, h_ref, g_ref, ho_ref, no_ref):
    h = h_ref[...] + _dot(a_ref[...], w_ref[...])
    ho_ref[...] = h
    no_ref[...] = _rms(h, g_ref[...]).astype(no_ref.dtype)


def _resid_norm_call(a, w, h, g, tm=512):
    m, d = h.shape
    k = a.shape[1]
    return pl.pallas_call(
        _resid_norm_kernel,
        grid=(m // tm,),
        in_specs=[pl.BlockSpec((tm, k), lambda i: (i, 0)),
                  pl.BlockSpec((k, d), lambda i: (0, 0)),
                  pl.BlockSpec((tm, d), lambda i: (i, 0)),
                  pl.BlockSpec((1, d), lambda i: (0, 0))],
        out_specs=[pl.BlockSpec((tm, d), lambda i: (i, 0)),
                   pl.BlockSpec((tm, d), lambda i: (i, 0))],
        out_shape=[jax.ShapeDtypeStruct((m, d), F32), jax.ShapeDtypeStruct((m, d), BF16)],
        compiler_params=_cparams(("parallel",)),
        name="mix_out_resid",
    )(a, w, h, g)


def _ffn_in_kernel(x_ref, wg_ref, wu_ref, cw_ref, cb_ref, o_ref, pad_ref, *, rc):
    s = x_ref.shape[1]
    _zero_halo(pad_ref, s)
    for c in range(s // rc):
        rows = pl.ds(c * rc, rc)
        pad_ref[pl.ds(CONV_PAD + c * rc, rc), :] = _dot(x_ref[0, rows, :], wg_ref[...])
    w = cw_ref[...]
    bias = cb_ref[...]
    for c in range(s // rc):
        rows = pl.ds(c * rc, rc)
        gate = _dwconv3_rows(pad_ref, c, rc, w) + bias
        up = _dot(x_ref[0, rows, :], wu_ref[...])
        o_ref[0, rows, :] = (gate * jax.nn.sigmoid(gate) * up).astype(o_ref.dtype)


def _ffn_in_call(hn, w_ffn_in, conv_w, conv_b, *, d_ff, tn=512, rc=512):
    b, s, d = hn.shape
    nt = d_ff // tn
    kern = functools.partial(_ffn_in_kernel, rc=rc)
    return pl.pallas_call(
        kern,
        grid=(b, nt),
        in_specs=[pl.BlockSpec((1, s, d), lambda bi, j: (bi, 0, 0)),
                  pl.BlockSpec((d, tn), lambda bi, j: (0, j)),
                  pl.BlockSpec((d, tn), lambda bi, j: (0, nt + j)),
                  pl.BlockSpec((3, tn), lambda bi, j: (0, j)),
                  pl.BlockSpec((1, tn), lambda bi, j: (0, j))],
        out_specs=pl.BlockSpec((1, s, tn), lambda bi, j: (bi, 0, j)),
        out_shape=jax.ShapeDtypeStruct((b, s, d_ff), BF16),
        scratch_shapes=[pltpu.VMEM((s + 2 * CONV_PAD, tn), F32)],
        compiler_params=_cparams(("parallel", "arbitrary")),
        name="ffn_in",
    )(hn, w_ffn_in, w_ffn_in, conv_w, conv_b)


def _ffn_out_kernel(a_ref, w_ref, h_ref, g_ref, ho_ref, no_ref):
    kk = pl.program_id(1)

    @pl.when(kk == 0)
    def _():
        ho_ref[...] = h_ref[...]

    ho_ref[...] += _dot(a_ref[...], w_ref[...])

    @pl.when(kk == pl.num_programs(1) - 1)
    def _():
        no_ref[...] = _rms(ho_ref[...], g_ref[...]).astype(no_ref.dtype)


def _ffn_out_call(act, w, h, g, tm=512, tk=1408):
    m, d = h.shape
    k = act.shape[1]
    return pl.pallas_call(
        _ffn_out_kernel,
        grid=(m // tm, k // tk),
        in_specs=[pl.BlockSpec((tm, tk), lambda i, kk: (i, kk)),
                  pl.BlockSpec((tk, d), lambda i, kk: (kk, 0)),
                  pl.BlockSpec((tm, d), lambda i, kk: (i, 0)),
                  pl.BlockSpec((1, d), lambda i, kk: (0, 0))],
        out_specs=[pl.BlockSpec((tm, d), lambda i, kk: (i, 0)),
                   pl.BlockSpec((tm, d), lambda i, kk: (i, 0))],
        out_shape=[jax.ShapeDtypeStruct((m, d), F32), jax.ShapeDtypeStruct((m, d), BF16)],
        compiler_params=_cparams(("parallel", "arbitrary")),
        name="ffn_out_resid",
    )(act, w, h, g)


def _ple_kernel(n_ref, wg_ref, p_ref, wp_ref, h_ref, g_ref, *out_refs, last):
    gate = jax.nn.sigmoid(_dot(n_ref[...], wg_ref[...]))
    emb = _dot(p_ref[...].astype(BF16), wp_ref[...])
    h = h_ref[...] + gate * emb
    if last:
        out_refs[0][...] = _rms(h, g_ref[...])
    else:
        out_refs[0][...] = h
        out_refs[1][...] = _rms(h, g_ref[...]).astype(out_refs[1].dtype)


def _ple_call(pn, w_gate, p, layer, w_ple, h, g, *, last, tm=512):
    m, d = h.shape
    pd = p.shape[-1]
    row = lambda i: (i, 0)
    const = lambda i: (0, 0)
    if last:
        out_specs = [pl.BlockSpec((tm, d), row)]
        out_shape = [jax.ShapeDtypeStruct((m, d), F32)]
    else:
        out_specs = [pl.BlockSpec((tm, d), row), pl.BlockSpec((tm, d), row)]
        out_shape = [jax.ShapeDtypeStruct((m, d), F32), jax.ShapeDtypeStruct((m, d), BF16)]
    return pl.pallas_call(
        functools.partial(_ple_kernel, last=last),
        grid=(m // tm,),
        in_specs=[pl.BlockSpec((tm, d), row),
                  pl.BlockSpec((d, d), const),
                  pl.BlockSpec((None, tm, pd), lambda i: (layer, i, 0)),
                  pl.BlockSpec((pd, d), const),
                  pl.BlockSpec((tm, d), row),
                  pl.BlockSpec((1, d), const)],
        out_specs=out_specs,
        out_shape=out_shape,
        compiler_params=_cparams(("parallel",)),
        name="ple_final" if last else "ple",
    )(pn, w_gate, p, w_ple, h, g)


def _rope_tables(seq):
    axis_dim = HEAD_DIM // 2
    rows = seq // GRID_W
    row_ids = jnp.repeat(jnp.arange(rows, dtype=F32), GRID_W)
    col_ids = jnp.tile(jnp.arange(GRID_W, dtype=F32), rows)
    inv_freq = 1.0 / (ROPE_THETA ** (jnp.arange(0, axis_dim, 2, dtype=F32) / axis_dim))
    ang_r = row_ids[:, None] * inv_freq[None, :]
    ang_c = col_ids[:, None] * inv_freq[None, :]
    ang = jnp.concatenate([ang_r, ang_r, ang_c, ang_c], axis=-1)
    cos, sin = jnp.cos(ang), jnp.sin(ang)
    first = (jnp.arange(HEAD_DIM) % axis_dim) < (axis_dim // 2)
    sin_a = jnp.where(first[None, :], -sin, 0.0)
    sin_b = jnp.where(first[None, :], 0.0, sin)
    return cos, sin_a, sin_b


def kernel(x, p, norm_mix_g, w_in, q_norm_g, k_norm_g, w_attn_br, conv_mix_w, conv_mix_b, w_conv_br, w_mix_out, norm_ffn_g, w_ffn_in, conv_ffn_w, conv_ffn_b, w_ffn_out, norm_ple_g, w_ple_gate, w_ple, final_norm_g):
    b, s, d = x.shape
    depth = w_in.shape[0]
    m = b * s
    n_kv = d // HEAD_DIM // Q_PER_KV
    kv_dim = n_kv * HEAD_DIM
    conv_dim = conv_mix_w.shape[-1]
    d_ff = conv_ffn_w.shape[-1]
    qkv_cols = d + 2 * kv_dim
    conv_col0 = qkv_cols
    gate_col0 = conv_col0 + 3 * conv_dim
    pd = p.shape[-1]

    cos, sin_a, sin_b = _rope_tables(s)
    p2 = p.reshape(depth, m, pd)

    h = x.reshape(m, d)
    xn = _norm_call(h, norm_mix_g[0].reshape(1, d))
    out = None
    for i in range(depth):
        w_in_i = w_in[i].astype(BF16)
        qkv = _qkv_call(xn.reshape(b, s, d), w_in_i, cos, sin_a, sin_b,
                        q_norm_g[i].reshape(1, HEAD_DIM), k_norm_g[i].reshape(1, HEAD_DIM),
                        n_cols=qkv_cols)
        conv = _convbr_call(xn.reshape(b, s, d), w_in_i, conv_mix_w[i], conv_mix_b[i].reshape(1, conv_dim),
                            col0=conv_col0, width=conv_dim)
        attn = _attn_call(qkv, d_model=d, n_kv=n_kv)
        merged = _merge_call(xn, attn.reshape(m, d), conv.reshape(m, conv_dim), w_in_i,
                             w_attn_br[i].astype(BF16), w_conv_br[i].astype(BF16), ga_col0=gate_col0)
        h, hn = _resid_norm_call(merged, w_mix_out[i].astype(BF16), h, norm_ffn_g[i].reshape(1, d))
        act = _ffn_in_call(hn.reshape(b, s, d), w_ffn_in[i].astype(BF16), conv_ffn_w[i],
                           conv_ffn_b[i].reshape(1, d_ff), d_ff=d_ff)
        h, pn = _ffn_out_call(act.reshape(m, d_ff), w_ffn_out[i].astype(BF16), h, norm_ple_g[i].reshape(1, d))
        last = i == depth - 1
        g_next = final_norm_g if last else norm_mix_g[i + 1]
        res = _ple_call(pn, w_ple_gate[i].astype(BF16), p2, i, w_ple[i].astype(BF16), h,
                        g_next.reshape(1, d), last=last)
        if last:
            out = res[0]
        else:
            h, xn = res
    return out.reshape(b, s, d)
```

```python
import functools

import numpy as np
import jax
import jax.numpy as jnp
from jax import lax
from jax.experimental import pallas as pl
from jax.experimental.pallas import tpu as pltpu

HEAD_DIM = 128
Q_PER_KV = 4
ROPE_THETA = 10000.0
GRID_W = 64
EPS = 1e-6
LOG2E = 1.4426950408889634
CONV_PAD = 8
V7X_VMEM_LIMIT = 60 * 1024 * 1024

F32 = jnp.float32
BF16 = jnp.bfloat16
SINGLE = pl.Buffered(1)


def _cparams(n_axes):
    return pltpu.CompilerParams(dimension_semantics=("arbitrary",) * n_axes,
                                vmem_limit_bytes=V7X_VMEM_LIMIT)


def _rms(x, g):
    ms = jnp.mean(x * x, axis=-1, keepdims=True)
    return x * lax.rsqrt(ms + EPS) * g


def _dot(a, b):
    return jnp.dot(a, b, preferred_element_type=F32)


def _cast_first(step, w_ref, wb_ref):
    @pl.when(step == 0)
    def _():
        wb_ref[...] = w_ref[...].astype(wb_ref.dtype)


def _norm_kernel(x_ref, g_ref, o_ref):
    o_ref[...] = _rms(x_ref[...], g_ref[...]).astype(o_ref.dtype)


def _norm_call(x, g, tm=512):
    m, d = x.shape
    return pl.pallas_call(
        _norm_kernel,
        grid=(m // tm,),
        in_specs=[pl.BlockSpec((tm, d), lambda i: (i, 0)),
                  pl.BlockSpec((1, d), lambda i: (0, 0))],
        out_specs=pl.BlockSpec((tm, d), lambda i: (i, 0)),
        out_shape=jax.ShapeDtypeStruct((m, d), BF16),
        compiler_params=_cparams(1),
        name="init_norm",
    )(x, g)


def _swap_mid(x):
    lane = lax.broadcasted_iota(jnp.int32, x.shape, x.ndim - 1)
    seg = lane // (HEAD_DIM // 4)
    up = pltpu.roll(x, 3 * HEAD_DIM // 4, x.ndim - 1)
    dn = pltpu.roll(x, HEAD_DIM // 4, x.ndim - 1)
    return jnp.where(seg == 1, up, jnp.where(seg == 2, dn, x))


def _cast_swapped_first(step, w_ref, wb_ref):
    @pl.when(step == 0)
    def _():
        for hh in range(w_ref.shape[1] // HEAD_DIM):
            cols = slice(hh * HEAD_DIM, (hh + 1) * HEAD_DIM)
            wb_ref[:, cols] = _swap_mid(w_ref[:, cols]).astype(wb_ref.dtype)


def _rope_gain_tables(g_ref, scale, cos, sgn_sin):
    g = _swap_mid(jnp.broadcast_to(g_ref[...], (8, HEAD_DIM)))[0:1] * scale
    return cos * g, sgn_sin * pltpu.roll(jnp.broadcast_to(g, (8, HEAD_DIM)), HEAD_DIM // 2, 1)[0:1]


def _head_mean_matrix():
    r = lax.broadcasted_iota(jnp.int32, (2 * HEAD_DIM, 2 * HEAD_DIM), 0) // HEAD_DIM
    c = lax.broadcasted_iota(jnp.int32, (2 * HEAD_DIM, 2 * HEAD_DIM), 1) // HEAD_DIM
    return jnp.where(r == c, 1.0 / HEAD_DIM, 0.0).astype(BF16)


def _norm_rope_heads(z, ca, sb, bd, o_ref, rows):
    z2 = (z * z).astype(BF16)
    for pair in range(z.shape[1] // (2 * HEAD_DIM)):
        rs = lax.rsqrt(_dot(z2[:, pair * 2 * HEAD_DIM:(pair + 1) * 2 * HEAD_DIM], bd) + EPS)
        for sub in range(2):
            hh = 2 * pair + sub
            x = z[:, hh * HEAD_DIM:(hh + 1) * HEAD_DIM]
            y = (x * ca + pltpu.roll(x, HEAD_DIM // 2, 1) * sb) * rs[:, sub * HEAD_DIM:(sub + 1) * HEAD_DIM]
            o_ref[0, hh, rows, :] = y.astype(o_ref.dtype)


def _q_kernel(x_ref, w_ref, cos_ref, sin_ref, g_ref, o_ref, wb_ref, *, rc):
    _cast_swapped_first(pl.program_id(1), w_ref, wb_ref)
    s = x_ref.shape[1]
    scale = LOG2E / float(np.sqrt(HEAD_DIM))
    bd = _head_mean_matrix()
    for c in range(s // rc):
        rows = pl.ds(c * rc, rc)
        ca, sb = _rope_gain_tables(g_ref, scale, cos_ref[rows, :], sin_ref[rows, :])
        _norm_rope_heads(_dot(x_ref[0, rows, :], wb_ref[...]), ca, sb, bd, o_ref, rows)


def _q_call(xn, w_in, layer, cos, sin, qg, *, tn=512, rc=256):
    b, s, d = xn.shape
    hpt = tn // HEAD_DIM
    const = lambda j, bi: (0, 0)
    return pl.pallas_call(
        functools.partial(_q_kernel, rc=rc),
        grid=(d // tn, b),
        in_specs=[pl.BlockSpec((1, s, d), lambda j, bi: (bi, 0, 0)),
                  pl.BlockSpec((None, d, tn), lambda j, bi: (layer, 0, j)),
                  pl.BlockSpec((s, HEAD_DIM), const),
                  pl.BlockSpec((s, HEAD_DIM), const),
                  pl.BlockSpec((1, HEAD_DIM), const)],
        out_specs=pl.BlockSpec((1, hpt, s, HEAD_DIM), lambda j, bi: (bi, j, 0, 0)),
        out_shape=jax.ShapeDtypeStruct((b, d // HEAD_DIM, s, HEAD_DIM), BF16),
        scratch_shapes=[pltpu.VMEM((d, tn), BF16)],
        compiler_params=_cparams(2),
        name="q_proj",
    )(xn, w_in, cos, sin, qg)


def _kv_kernel(x_ref, wk_ref, wv_ref, cos_ref, sin_ref, g_ref, k_ref, v_ref, wkb_ref, wvb_ref, *, rc):
    _cast_swapped_first(pl.program_id(0), wk_ref, wkb_ref)
    _cast_first(pl.program_id(0), wv_ref, wvb_ref)
    s = x_ref.shape[1]
    n_kv = k_ref.shape[1]
    ones = jnp.ones((rc, HEAD_DIM), v_ref.dtype)
    bd = _head_mean_matrix()
    for c in range(s // rc):
        rows = pl.ds(c * rc, rc)
        x = x_ref[0, rows, :]
        ca, sb = _rope_gain_tables(g_ref, 1.0, cos_ref[rows, :], sin_ref[rows, :])
        _norm_rope_heads(_dot(x, wkb_ref[...]), ca, sb, bd, k_ref, rows)
        zv = _dot(x, wvb_ref[...])
        for hh in range(n_kv):
            v_ref[0, hh, rows, :HEAD_DIM] = zv[:, hh * HEAD_DIM:(hh + 1) * HEAD_DIM].astype(v_ref.dtype)
            v_ref[0, hh, rows, HEAD_DIM:] = ones


def _kv_call(xn, w_in, layer, cos, sin, kg, *, k_col0, n_kv, rc=256):
    b, s, d = xn.shape
    kv = n_kv * HEAD_DIM
    kt = k_col0 // kv
    const = lambda bi: (0, 0)
    return pl.pallas_call(
        functools.partial(_kv_kernel, rc=rc),
        grid=(b,),
        in_specs=[pl.BlockSpec((1, s, d), lambda bi: (bi, 0, 0)),
                  pl.BlockSpec((None, d, kv), lambda bi: (layer, 0, kt), pipeline_mode=SINGLE),
                  pl.BlockSpec((None, d, kv), lambda bi: (layer, 0, kt + 1), pipeline_mode=SINGLE),
                  pl.BlockSpec((s, HEAD_DIM), const),
                  pl.BlockSpec((s, HEAD_DIM), const),
                  pl.BlockSpec((1, HEAD_DIM), const)],
        out_specs=[pl.BlockSpec((1, n_kv, s, HEAD_DIM), lambda bi: (bi, 0, 0, 0)),
                   pl.BlockSpec((1, n_kv, s, 2 * HEAD_DIM), lambda bi: (bi, 0, 0, 0))],
        out_shape=[jax.ShapeDtypeStruct((b, n_kv, s, HEAD_DIM), BF16),
                   jax.ShapeDtypeStruct((b, n_kv, s, 2 * HEAD_DIM), BF16)],
        scratch_shapes=[pltpu.VMEM((d, kv), BF16), pltpu.VMEM((d, kv), BF16)],
        compiler_params=_cparams(1),
        name="kv_proj",
    )(xn, w_in, w_in, cos, sin, kg)


def _dwconv3_rows(pad_ref, c, rc, w):
    base = CONV_PAD + c * rc
    prev = pad_ref[pl.ds(base - 1, rc), :]
    cur = pad_ref[pl.ds(base, rc), :]
    nxt = pad_ref[pl.ds(base + 1, rc), :]
    return prev * w[0:1, :] + cur * w[1:2, :] + nxt * w[2:3, :]


def _zero_halo(pad_ref, s):
    zeros = jnp.zeros((CONV_PAD, pad_ref.shape[1]), pad_ref.dtype)
    pad_ref[pl.ds(0, CONV_PAD), :] = zeros
    pad_ref[pl.ds(CONV_PAD + s, CONV_PAD), :] = zeros


def _convbr_kernel(x_ref, wb_ref, wc_ref, wu_ref, cw_ref, cb_ref, o_ref, wbb_ref, wcb_ref, wub_ref, pad_ref,
                   *, rc):
    step = pl.program_id(1)
    _cast_first(step, wb_ref, wbb_ref)
    _cast_first(step, wc_ref, wcb_ref)
    _cast_first(step, wu_ref, wub_ref)
    s = x_ref.shape[1]
    _zero_halo(pad_ref, s)
    for c in range(s // rc):
        x = x_ref[0, pl.ds(c * rc, rc), :]
        pad_ref[pl.ds(CONV_PAD + c * rc, rc), :] = _dot(x, wcb_ref[...]) * _dot(x, wub_ref[...])
    w = cw_ref[...]
    bias = cb_ref[...]
    for c in range(s // rc):
        rows = pl.ds(c * rc, rc)
        conv = _dwconv3_rows(pad_ref, c, rc, w) + bias
        bg = _dot(x_ref[0, rows, :], wbb_ref[...])
        o_ref[0, rows, :] = (bg * conv).astype(o_ref.dtype)


def _convbr_call(xn, w_in, layer, conv_w, conv_b, *, col0, width, tn=256, rc=512):
    b, s, d = xn.shape
    t0 = col0 // tn
    nt = width // tn
    wspec = lambda off: pl.BlockSpec((None, d, tn), lambda j, bi: (layer, 0, t0 + off * nt + j))
    return pl.pallas_call(
        functools.partial(_convbr_kernel, rc=rc),
        grid=(nt, b),
        in_specs=[pl.BlockSpec((1, s, d), lambda j, bi: (bi, 0, 0)),
                  wspec(0), wspec(1), wspec(2),
                  pl.BlockSpec((3, tn), lambda j, bi: (0, j)),
                  pl.BlockSpec((1, tn), lambda j, bi: (0, j))],
        out_specs=pl.BlockSpec((1, s, tn), lambda j, bi: (bi, 0, j)),
        out_shape=jax.ShapeDtypeStruct((b, s, width), BF16),
        scratch_shapes=[pltpu.VMEM((d, tn), BF16), pltpu.VMEM((d, tn), BF16), pltpu.VMEM((d, tn), BF16),
                        pltpu.VMEM((s + 2 * CONV_PAD, tn), F32)],
        compiler_params=_cparams(2),
        name="conv_branch",
    )(xn, w_in, w_in, w_in, conv_w, conv_b)


def _attn_kernel(q_ref, k_ref, v_ref, o_ref, *, rc):
    k = k_ref[0, 0]
    v = v_ref[0, 0]
    tq = q_ref.shape[2]
    for g in range(Q_PER_KV):
        for r in range(tq // rc):
            rows = pl.ds(r * rc, rc)
            sc = lax.dot_general(q_ref[0, g, rows, :], k, (((1,), (1,)), ((), ())),
                                 preferred_element_type=F32)
            m = jnp.max(sc, axis=-1, keepdims=True)
            p = jnp.exp2(sc - m).astype(BF16)
            oe = _dot(p, v)
            o = oe[:, :HEAD_DIM] / oe[:, HEAD_DIM:]
            o_ref[0, rows, g * HEAD_DIM:(g + 1) * HEAD_DIM] = o.astype(o_ref.dtype)


def _attn_call(q, k, v, *, tq=1024, rc=256):
    b, n_q, s, _ = q.shape
    n_kv = k.shape[1]
    gw = Q_PER_KV * HEAD_DIM
    return pl.pallas_call(
        functools.partial(_attn_kernel, rc=rc),
        grid=(b, n_kv, s // tq),
        in_specs=[pl.BlockSpec((1, Q_PER_KV, tq, HEAD_DIM), lambda bi, h, i: (bi, h, i, 0)),
                  pl.BlockSpec((1, 1, s, HEAD_DIM), lambda bi, h, i: (bi, h, 0, 0)),
                  pl.BlockSpec((1, 1, s, 2 * HEAD_DIM), lambda bi, h, i: (bi, h, 0, 0))],
        out_specs=pl.BlockSpec((1, tq, gw), lambda bi, h, i: (bi, i, h)),
        out_shape=jax.ShapeDtypeStruct((b, s, n_q * HEAD_DIM), BF16),
        compiler_params=_cparams(3),
        name="gqa_attention",
    )(q, k, v)


def _merge_kernel(x_ref, a_ref, c_ref, wga_ref, wgc_ref, wa_ref, wc_ref, o_ref,
                  wgab_ref, wgcb_ref, wab_ref, wcb_ref):
    step = pl.program_id(1)
    _cast_first(step, wga_ref, wgab_ref)
    _cast_first(step, wgc_ref, wgcb_ref)
    _cast_first(step, wa_ref, wab_ref)
    _cast_first(step, wc_ref, wcb_ref)
    x = x_ref[...]
    ga = jax.nn.sigmoid(_dot(x, wgab_ref[...]))
    gc = jax.nn.sigmoid(_dot(x, wgcb_ref[...]))
    merged = ga * _dot(a_ref[...], wab_ref[...]) + gc * _dot(c_ref[...], wcb_ref[...])
    o_ref[...] = merged.astype(o_ref.dtype)


def _merge_call(xn, attn, conv, w_in, layer, w_attn_br, w_conv_br, *, ga_col0, tm=512, tn=512):
    m, d = xn.shape
    cdim = conv.shape[1]
    ga0 = ga_col0 // tn
    gc0 = ga0 + d // tn
    return pl.pallas_call(
        _merge_kernel,
        grid=(d // tn, m // tm),
        in_specs=[pl.BlockSpec((tm, d), lambda j, i: (i, 0)),
                  pl.BlockSpec((tm, d), lambda j, i: (i, 0)),
                  pl.BlockSpec((tm, cdim), lambda j, i: (i, 0)),
                  pl.BlockSpec((None, d, tn), lambda j, i: (layer, 0, ga0 + j)),
                  pl.BlockSpec((None, d, tn), lambda j, i: (layer, 0, gc0 + j)),
                  pl.BlockSpec((None, d, tn), lambda j, i: (layer, 0, j)),
                  pl.BlockSpec((None, cdim, tn), lambda j, i: (layer, 0, j))],
        out_specs=pl.BlockSpec((tm, tn), lambda j, i: (i, j)),
        out_shape=jax.ShapeDtypeStruct((m, d), BF16),
        scratch_shapes=[pltpu.VMEM((d, tn), BF16), pltpu.VMEM((d, tn), BF16), pltpu.VMEM((d, tn), BF16),
                        pltpu.VMEM((cdim, tn), BF16)],
        compiler_params=_cparams(2),
        name="gate_merge",
    )(xn, attn, conv, w_in, w_in, w_attn_br, w_conv_br)


def _resid_norm_kernel(a_ref, w_ref, h_ref, g_ref, ho_ref, no_ref, wb_ref, *, rc):
    _cast_first(pl.program_id(0), w_ref, wb_ref)
    g = g_ref[...]
    for c in range(a_ref.shape[0] // rc):
        rows = pl.ds(c * rc, rc)
        h = h_ref[rows, :] + _dot(a_ref[rows, :], wb_ref[...])
        ho_ref[rows, :] = h
        no_ref[rows, :] = _rms(h, g).astype(no_ref.dtype)


def _resid_norm_call(a, w, layer, h, g, tm=512, rc=256):
    m, d = h.shape
    k = a.shape[1]
    row = lambda i: (i, 0)
    return pl.pallas_call(
        functools.partial(_resid_norm_kernel, rc=rc),
        grid=(m // tm,),
        in_specs=[pl.BlockSpec((tm, k), row),
                  pl.BlockSpec((None, k, d), lambda i: (layer, 0, 0), pipeline_mode=SINGLE),
                  pl.BlockSpec((tm, d), row),
                  pl.BlockSpec((1, d), lambda i: (0, 0))],
        out_specs=[pl.BlockSpec((tm, d), row), pl.BlockSpec((tm, d), row)],
        out_shape=[jax.ShapeDtypeStruct((m, d), F32), jax.ShapeDtypeStruct((m, d), BF16)],
        scratch_shapes=[pltpu.VMEM((k, d), BF16)],
        compiler_params=_cparams(1),
        name="mix_out_resid",
    )(a, w, h, g)


def _ffn_in_kernel(x_ref, wg_ref, wu_ref, cw_ref, cb_ref, o_ref, wgb_ref, wub_ref, pad_ref, *, rc):
    step = pl.program_id(1)
    _cast_first(step, wg_ref, wgb_ref)
    _cast_first(step, wu_ref, wub_ref)
    s = x_ref.shape[1]
    _zero_halo(pad_ref, s)
    for c in range(s // rc):
        pad_ref[pl.ds(CONV_PAD + c * rc, rc), :] = _dot(x_ref[0, pl.ds(c * rc, rc), :], wgb_ref[...])
    w = cw_ref[...]
    bias = cb_ref[...]
    for c in range(s // rc):
        rows = pl.ds(c * rc, rc)
        gate = _dwconv3_rows(pad_ref, c, rc, w) + bias
        up = _dot(x_ref[0, rows, :], wub_ref[...])
        o_ref[0, rows, :] = (gate * jax.nn.sigmoid(gate) * up).astype(o_ref.dtype)


def _ffn_in_call(hn, w_ffn_in, layer, conv_w, conv_b, *, d_ff, tn=512, rc=512):
    b, s, d = hn.shape
    nt = d_ff // tn
    return pl.pallas_call(
        functools.partial(_ffn_in_kernel, rc=rc),
        grid=(nt, b),
        in_specs=[pl.BlockSpec((1, s, d), lambda j, bi: (bi, 0, 0)),
                  pl.BlockSpec((None, d, tn), lambda j, bi: (layer, 0, j)),
                  pl.BlockSpec((None, d, tn), lambda j, bi: (layer, 0, nt + j)),
                  pl.BlockSpec((3, tn), lambda j, bi: (0, j)),
                  pl.BlockSpec((1, tn), lambda j, bi: (0, j))],
        out_specs=pl.BlockSpec((1, s, tn), lambda j, bi: (bi, 0, j)),
        out_shape=jax.ShapeDtypeStruct((b, s, d_ff), BF16),
        scratch_shapes=[pltpu.VMEM((d, tn), BF16), pltpu.VMEM((d, tn), BF16),
                        pltpu.VMEM((s + 2 * CONV_PAD, tn), F32)],
        compiler_params=_cparams(2),
        name="ffn_in",
    )(hn, w_ffn_in, w_ffn_in, conv_w, conv_b)


def _ffn_out_kernel(a_ref, w_ref, h_ref, ho_ref, wb_ref):
    _cast_first(pl.program_id(1), w_ref, wb_ref)
    ho_ref[...] = h_ref[...] + _dot(a_ref[...], wb_ref[...])


def _ffn_out_call(act, w, layer, h, tm=512, tn=512):
    m, d = h.shape
    k = act.shape[1]
    return pl.pallas_call(
        _ffn_out_kernel,
        grid=(d // tn, m // tm),
        in_specs=[pl.BlockSpec((tm, k), lambda j, i: (i, 0)),
                  pl.BlockSpec((None, k, tn), lambda j, i: (layer, 0, j)),
                  pl.BlockSpec((tm, tn), lambda j, i: (i, j))],
        out_specs=pl.BlockSpec((tm, tn), lambda j, i: (i, j)),
        out_shape=jax.ShapeDtypeStruct((m, d), F32),
        scratch_shapes=[pltpu.VMEM((k, tn), BF16)],
        compiler_params=_cparams(2),
        name="ffn_out_resid",
    )(act, w, h)


def _ple_kernel(h_ref, gp_ref, wg_ref, p_ref, wp_ref, gn_ref, *refs, last, rc):
    out_refs, (wgb_ref, wpb_ref) = refs[:-2], refs[-2:]
    _cast_first(pl.program_id(0), wg_ref, wgb_ref)
    _cast_first(pl.program_id(0), wp_ref, wpb_ref)
    gp = gp_ref[...]
    gn = gn_ref[...]
    for c in range(h_ref.shape[0] // rc):
        rows = pl.ds(c * rc, rc)
        h = h_ref[rows, :]
        gate = jax.nn.sigmoid(_dot(_rms(h, gp).astype(BF16), wgb_ref[...]))
        emb = _dot(p_ref[rows, :].astype(BF16), wpb_ref[...])
        h = h + gate * emb
        if last:
            out_refs[0][rows, :] = _rms(h, gn)
        else:
            out_refs[0][rows, :] = h
            out_refs[1][rows, :] = _rms(h, gn).astype(out_refs[1].dtype)


def _ple_call(h, g_ple, w_gate, p, layer, w_ple, g_next, *, last, tm=256, rc=128):
    m, d = h.shape
    pd = p.shape[-1]
    row = lambda i: (i, 0)
    const = lambda i: (0, 0)
    if last:
        out_specs = [pl.BlockSpec((tm, d), row)]
        out_shape = [jax.ShapeDtypeStruct((m, d), F32)]
    else:
        out_specs = [pl.BlockSpec((tm, d), row), pl.BlockSpec((tm, d), row)]
        out_shape = [jax.ShapeDtypeStruct((m, d), F32), jax.ShapeDtypeStruct((m, d), BF16)]
    return pl.pallas_call(
        functools.partial(_ple_kernel, last=last, rc=rc),
        grid=(m // tm,),
        in_specs=[pl.BlockSpec((tm, d), row),
                  pl.BlockSpec((1, d), const),
                  pl.BlockSpec((None, d, d), lambda i: (layer, 0, 0), pipeline_mode=SINGLE),
                  pl.BlockSpec((None, tm, pd), lambda i: (layer, i, 0)),
                  pl.BlockSpec((None, pd, d), lambda i: (layer, 0, 0), pipeline_mode=SINGLE),
                  pl.BlockSpec((1, d), const)],
        out_specs=out_specs,
        out_shape=out_shape,
        scratch_shapes=[pltpu.VMEM((d, d), BF16), pltpu.VMEM((pd, d), BF16)],
        compiler_params=_cparams(1),
        name="ple_final" if last else "ple",
    )(h, g_ple, w_gate, p, w_ple, g_next)


def _rope_tables(seq):
    axis_dim = HEAD_DIM // 2
    rows = seq // GRID_W
    row_ids = jnp.repeat(jnp.arange(rows, dtype=F32), GRID_W)
    col_ids = jnp.tile(jnp.arange(GRID_W, dtype=F32), rows)
    inv_freq = 1.0 / (ROPE_THETA ** (jnp.arange(0, axis_dim, 2, dtype=F32) / axis_dim))
    ang_r = row_ids[:, None] * inv_freq[None, :]
    ang_c = col_ids[:, None] * inv_freq[None, :]
    ang = jnp.concatenate([ang_r, ang_c, ang_r, ang_c], axis=-1)
    sign = jnp.where(jnp.arange(HEAD_DIM) < axis_dim, -1.0, 1.0).astype(F32)
    return jnp.cos(ang), jnp.sin(ang) * sign[None, :]


def kernel(x, p, norm_mix_g, w_in, q_norm_g, k_norm_g, w_attn_br, conv_mix_w, conv_mix_b, w_conv_br, w_mix_out, norm_ffn_g, w_ffn_in, conv_ffn_w, conv_ffn_b, w_ffn_out, norm_ple_g, w_ple_gate, w_ple, final_norm_g):
    b, s, d = x.shape
    depth = w_in.shape[0]
    m = b * s
    n_kv = d // HEAD_DIM // Q_PER_KV
    kv_dim = n_kv * HEAD_DIM
    conv_dim = conv_mix_w.shape[-1]
    d_ff = conv_ffn_w.shape[-1]
    conv_col0 = d + 2 * kv_dim
    gate_col0 = conv_col0 + 3 * conv_dim
    pd = p.shape[-1]

    cos, sin = _rope_tables(s)
    p2 = p.reshape(depth, m, pd)

    h = x.reshape(m, d)
    xn = _norm_call(h, norm_mix_g[0].reshape(1, d))
    out = None
    for i in range(depth):
        xn3 = xn.reshape(b, s, d)
        q = _q_call(xn3, w_in, i, cos, sin, q_norm_g[i].reshape(1, HEAD_DIM))
        k, v = _kv_call(xn3, w_in, i, cos, sin, k_norm_g[i].reshape(1, HEAD_DIM), k_col0=d, n_kv=n_kv)
        conv = _convbr_call(xn3, w_in, i, conv_mix_w[i], conv_mix_b[i].reshape(1, conv_dim),
                            col0=conv_col0, width=conv_dim)
        attn = _attn_call(q, k, v)
        merged = _merge_call(xn, attn.reshape(m, d), conv.reshape(m, conv_dim), w_in, i,
                             w_attn_br, w_conv_br, ga_col0=gate_col0)
        h, hn = _resid_norm_call(merged, w_mix_out, i, h, norm_ffn_g[i].reshape(1, d))
        act = _ffn_in_call(hn.reshape(b, s, d), w_ffn_in, i, conv_ffn_w[i],
                           conv_ffn_b[i].reshape(1, d_ff), d_ff=d_ff)
        h = _ffn_out_call(act.reshape(m, d_ff), w_ffn_out, i, h)
        last = i == depth - 1
        g_next = final_norm_g if last else norm_mix_g[i + 1]
        res = _ple_call(h, norm_ple_g[i].reshape(1, d), w_ple_gate, p2, i, w_ple, g_next.reshape(1, d),
                        last=last)
        if last:
            out = res[0]
        else:
            h, xn = res
    return out.reshape(b, s, d)
```

```python
import functools

import numpy as np
import jax
import jax.numpy as jnp
from jax import lax
from jax.experimental import pallas as pl
from jax.experimental.pallas import tpu as pltpu

HEAD_DIM = 128
Q_PER_KV = 4
ROPE_THETA = 10000.0
GRID_W = 64
EPS = 1e-6
LOG2E = 1.4426950408889634
CONV_PAD = 8
V7X_VMEM_LIMIT = 60 * 1024 * 1024

F32 = jnp.float32
BF16 = jnp.bfloat16
SINGLE = pl.Buffered(1)


def _cparams(n_axes):
    return pltpu.CompilerParams(dimension_semantics=("arbitrary",) * n_axes,
                                vmem_limit_bytes=V7X_VMEM_LIMIT)


def _rms(x, g):
    ms = jnp.mean(x * x, axis=-1, keepdims=True)
    return x * lax.rsqrt(ms + EPS) * g


def _dot(a, b):
    return jnp.dot(a, b, preferred_element_type=F32)


def _cast_first(step, w_ref, wb_ref):
    @pl.when(step == 0)
    def _():
        wb_ref[...] = w_ref[...].astype(wb_ref.dtype)


def _norm_kernel(x_ref, g_ref, o_ref):
    o_ref[...] = _rms(x_ref[...], g_ref[...]).astype(o_ref.dtype)


def _norm_call(x, g, tm=512):
    m, d = x.shape
    return pl.pallas_call(
        _norm_kernel,
        grid=(m // tm,),
        in_specs=[pl.BlockSpec((tm, d), lambda i: (i, 0)),
                  pl.BlockSpec((1, d), lambda i: (0, 0))],
        out_specs=pl.BlockSpec((tm, d), lambda i: (i, 0)),
        out_shape=jax.ShapeDtypeStruct((m, d), BF16),
        compiler_params=_cparams(1),
        name="init_norm",
    )(x, g)


def _swap_mid(x):
    lane = lax.broadcasted_iota(jnp.int32, x.shape, x.ndim - 1)
    seg = lane // (HEAD_DIM // 4)
    up = pltpu.roll(x, 3 * HEAD_DIM // 4, x.ndim - 1)
    dn = pltpu.roll(x, HEAD_DIM // 4, x.ndim - 1)
    return jnp.where(seg == 1, up, jnp.where(seg == 2, dn, x))


def _cast_swapped_first(step, w_ref, wb_ref):
    @pl.when(step == 0)
    def _():
        for hh in range(w_ref.shape[1] // HEAD_DIM):
            cols = slice(hh * HEAD_DIM, (hh + 1) * HEAD_DIM)
            wb_ref[:, cols] = _swap_mid(w_ref[:, cols]).astype(wb_ref.dtype)


def _rope_gain_tables(g_ref, scale, cos, sgn_sin):
    g = _swap_mid(jnp.broadcast_to(g_ref[...], (8, HEAD_DIM)))[0:1] * scale
    return cos * g, sgn_sin * pltpu.roll(jnp.broadcast_to(g, (8, HEAD_DIM)), HEAD_DIM // 2, 1)[0:1]


def _head_mean_matrix():
    r = lax.broadcasted_iota(jnp.int32, (2 * HEAD_DIM, 2 * HEAD_DIM), 0) // HEAD_DIM
    c = lax.broadcasted_iota(jnp.int32, (2 * HEAD_DIM, 2 * HEAD_DIM), 1) // HEAD_DIM
    return jnp.where(r == c, 1.0 / HEAD_DIM, 0.0).astype(BF16)


def _norm_rope_heads(z, ca, sb, bd, o_ref, rows):
    z2 = (z * z).astype(BF16)
    for pair in range(z.shape[1] // (2 * HEAD_DIM)):
        rs = lax.rsqrt(_dot(z2[:, pair * 2 * HEAD_DIM:(pair + 1) * 2 * HEAD_DIM], bd) + EPS)
        for sub in range(2):
            hh = 2 * pair + sub
            x = z[:, hh * HEAD_DIM:(hh + 1) * HEAD_DIM]
            y = (x * ca + pltpu.roll(x, HEAD_DIM // 2, 1) * sb) * rs[:, sub * HEAD_DIM:(sub + 1) * HEAD_DIM]
            o_ref[0, hh, rows, :] = y.astype(o_ref.dtype)


def _q_kernel(x_ref, w_ref, cos_ref, sin_ref, g_ref, o_ref, wb_ref, *, rc):
    _cast_swapped_first(pl.program_id(1), w_ref, wb_ref)
    s = x_ref.shape[1]
    scale = LOG2E / float(np.sqrt(HEAD_DIM))
    bd = _head_mean_matrix()
    for c in range(s // rc):
        rows = pl.ds(c * rc, rc)
        ca, sb = _rope_gain_tables(g_ref, scale, cos_ref[rows, :], sin_ref[rows, :])
        _norm_rope_heads(_dot(x_ref[0, rows, :], wb_ref[...]), ca, sb, bd, o_ref, rows)


def _q_call(xn, w_in, layer, cos, sin, qg, *, tn=1024, rc=256):
    b, s, d = xn.shape
    hpt = tn // HEAD_DIM
    const = lambda j, bi: (0, 0)
    return pl.pallas_call(
        functools.partial(_q_kernel, rc=rc),
        grid=(d // tn, b),
        in_specs=[pl.BlockSpec((1, s, d), lambda j, bi: (bi, 0, 0)),
                  pl.BlockSpec((None, d, tn), lambda j, bi: (layer, 0, j)),
                  pl.BlockSpec((s, HEAD_DIM), const),
                  pl.BlockSpec((s, HEAD_DIM), const),
                  pl.BlockSpec((1, HEAD_DIM), const)],
        out_specs=pl.BlockSpec((1, hpt, s, HEAD_DIM), lambda j, bi: (bi, j, 0, 0)),
        out_shape=jax.ShapeDtypeStruct((b, d // HEAD_DIM, s, HEAD_DIM), BF16),
        scratch_shapes=[pltpu.VMEM((d, tn), BF16)],
        compiler_params=_cparams(2),
        name="q_proj",
    )(xn, w_in, cos, sin, qg)


def _kv_kernel(x_ref, wk_ref, wv_ref, cos_ref, sin_ref, g_ref, k_ref, v_ref, wkb_ref, wvb_ref, *, rc):
    _cast_swapped_first(pl.program_id(0), wk_ref, wkb_ref)
    _cast_first(pl.program_id(0), wv_ref, wvb_ref)
    s = x_ref.shape[1]
    n_kv = k_ref.shape[1]
    ones = jnp.ones((rc, HEAD_DIM), v_ref.dtype)
    bd = _head_mean_matrix()
    for c in range(s // rc):
        rows = pl.ds(c * rc, rc)
        x = x_ref[0, rows, :]
        ca, sb = _rope_gain_tables(g_ref, 1.0, cos_ref[rows, :], sin_ref[rows, :])
        _norm_rope_heads(_dot(x, wkb_ref[...]), ca, sb, bd, k_ref, rows)
        zv = _dot(x, wvb_ref[...])
        for hh in range(n_kv):
            v_ref[0, hh, rows, :HEAD_DIM] = zv[:, hh * HEAD_DIM:(hh + 1) * HEAD_DIM].astype(v_ref.dtype)
            v_ref[0, hh, rows, HEAD_DIM:] = ones


def _kv_call(xn, w_in, layer, cos, sin, kg, *, k_col0, n_kv, rc=256):
    b, s, d = xn.shape
    kv = n_kv * HEAD_DIM
    kt = k_col0 // kv
    const = lambda bi: (0, 0)
    return pl.pallas_call(
        functools.partial(_kv_kernel, rc=rc),
        grid=(b,),
        in_specs=[pl.BlockSpec((1, s, d), lambda bi: (bi, 0, 0)),
                  pl.BlockSpec((None, d, kv), lambda bi: (layer, 0, kt), pipeline_mode=SINGLE),
                  pl.BlockSpec((None, d, kv), lambda bi: (layer, 0, kt + 1), pipeline_mode=SINGLE),
                  pl.BlockSpec((s, HEAD_DIM), const),
                  pl.BlockSpec((s, HEAD_DIM), const),
                  pl.BlockSpec((1, HEAD_DIM), const)],
        out_specs=[pl.BlockSpec((1, n_kv, s, HEAD_DIM), lambda bi: (bi, 0, 0, 0)),
                   pl.BlockSpec((1, n_kv, s, 2 * HEAD_DIM), lambda bi: (bi, 0, 0, 0))],
        out_shape=[jax.ShapeDtypeStruct((b, n_kv, s, HEAD_DIM), BF16),
                   jax.ShapeDtypeStruct((b, n_kv, s, 2 * HEAD_DIM), BF16)],
        scratch_shapes=[pltpu.VMEM((d, kv), BF16), pltpu.VMEM((d, kv), BF16)],
        compiler_params=_cparams(1),
        name="kv_proj",
    )(xn, w_in, w_in, cos, sin, kg)


def _dwconv3_rows(pad_ref, c, rc, w):
    base = CONV_PAD + c * rc
    prev = pad_ref[pl.ds(base - 1, rc), :]
    cur = pad_ref[pl.ds(base, rc), :]
    nxt = pad_ref[pl.ds(base + 1, rc), :]
    return prev * w[0:1, :] + cur * w[1:2, :] + nxt * w[2:3, :]


def _zero_halo(pad_ref, s):
    zeros = jnp.zeros((CONV_PAD, pad_ref.shape[1]), pad_ref.dtype)
    pad_ref[pl.ds(0, CONV_PAD), :] = zeros
    pad_ref[pl.ds(CONV_PAD + s, CONV_PAD), :] = zeros


def _convbr_kernel(x_ref, wb_ref, wc_ref, wu_ref, cw_ref, cb_ref, o_ref, wbb_ref, wcb_ref, wub_ref, pad_ref,
                   *, rc):
    step = pl.program_id(1)
    _cast_first(step, wb_ref, wbb_ref)
    _cast_first(step, wc_ref, wcb_ref)
    _cast_first(step, wu_ref, wub_ref)
    s = x_ref.shape[1]
    _zero_halo(pad_ref, s)
    for c in range(s // rc):
        x = x_ref[0, pl.ds(c * rc, rc), :]
        pad_ref[pl.ds(CONV_PAD + c * rc, rc), :] = _dot(x, wcb_ref[...]) * _dot(x, wub_ref[...])
    w = cw_ref[...]
    bias = cb_ref[...]
    for c in range(s // rc):
        rows = pl.ds(c * rc, rc)
        conv = _dwconv3_rows(pad_ref, c, rc, w) + bias
        bg = _dot(x_ref[0, rows, :], wbb_ref[...])
        o_ref[0, rows, :] = (bg * conv).astype(o_ref.dtype)


def _convbr_call(xn, w_in, layer, conv_w, conv_b, *, col0, width, tn=256, rc=512):
    b, s, d = xn.shape
    t0 = col0 // tn
    nt = width // tn
    wspec = lambda off: pl.BlockSpec((None, d, tn), lambda j, bi: (layer, 0, t0 + off * nt + j))
    return pl.pallas_call(
        functools.partial(_convbr_kernel, rc=rc),
        grid=(nt, b),
        in_specs=[pl.BlockSpec((1, s, d), lambda j, bi: (bi, 0, 0)),
                  wspec(0), wspec(1), wspec(2),
                  pl.BlockSpec((3, tn), lambda j, bi: (0, j)),
                  pl.BlockSpec((1, tn), lambda j, bi: (0, j))],
        out_specs=pl.BlockSpec((1, s, tn), lambda j, bi: (bi, 0, j)),
        out_shape=jax.ShapeDtypeStruct((b, s, width), BF16),
        scratch_shapes=[pltpu.VMEM((d, tn), BF16), pltpu.VMEM((d, tn), BF16), pltpu.VMEM((d, tn), BF16),
                        pltpu.VMEM((s + 2 * CONV_PAD, tn), F32)],
        compiler_params=_cparams(2),
        name="conv_branch",
    )(xn, w_in, w_in, w_in, conv_w, conv_b)


def _attn_kernel(q_ref, k_ref, v_ref, o_ref, *, rc):
    k = k_ref[0, 0]
    v = v_ref[0, 0]
    tq = q_ref.shape[2]
    for g in range(Q_PER_KV):
        for r in range(tq // rc):
            rows = pl.ds(r * rc, rc)
            sc = lax.dot_general(q_ref[0, g, rows, :], k, (((1,), (1,)), ((), ())),
                                 preferred_element_type=F32)
            m = jnp.max(sc, axis=-1, keepdims=True)
            p = jnp.exp2(sc - m).astype(BF16)
            oe = _dot(p, v)
            o = oe[:, :HEAD_DIM] / oe[:, HEAD_DIM:]
            o_ref[0, rows, g * HEAD_DIM:(g + 1) * HEAD_DIM] = o.astype(o_ref.dtype)


def _attn_call(q, k, v, *, tq=2048, rc=256):
    b, n_q, s, _ = q.shape
    n_kv = k.shape[1]
    gw = Q_PER_KV * HEAD_DIM
    return pl.pallas_call(
        functools.partial(_attn_kernel, rc=rc),
        grid=(b, n_kv, s // tq),
        in_specs=[pl.BlockSpec((1, Q_PER_KV, tq, HEAD_DIM), lambda bi, h, i: (bi, h, i, 0)),
                  pl.BlockSpec((1, 1, s, HEAD_DIM), lambda bi, h, i: (bi, h, 0, 0)),
                  pl.BlockSpec((1, 1, s, 2 * HEAD_DIM), lambda bi, h, i: (bi, h, 0, 0))],
        out_specs=pl.BlockSpec((1, tq, gw), lambda bi, h, i: (bi, i, h)),
        out_shape=jax.ShapeDtypeStruct((b, s, n_q * HEAD_DIM), BF16),
        compiler_params=_cparams(3),
        name="gqa_attention",
    )(q, k, v)


def _merge_kernel(x_ref, a_ref, c_ref, wga_ref, wgc_ref, wa_ref, wc_ref, o_ref,
                  wgab_ref, wgcb_ref, wab_ref, wcb_ref):
    step = pl.program_id(1)
    _cast_first(step, wga_ref, wgab_ref)
    _cast_first(step, wgc_ref, wgcb_ref)
    _cast_first(step, wa_ref, wab_ref)
    _cast_first(step, wc_ref, wcb_ref)
    x = x_ref[...]
    ga = jax.nn.sigmoid(_dot(x, wgab_ref[...]))
    gc = jax.nn.sigmoid(_dot(x, wgcb_ref[...]))
    merged = ga * _dot(a_ref[...], wab_ref[...]) + gc * _dot(c_ref[...], wcb_ref[...])
    o_ref[...] = merged.astype(o_ref.dtype)


def _merge_call(xn, attn, conv, w_in, layer, w_attn_br, w_conv_br, *, ga_col0, tm=512, tn=512):
    m, d = xn.shape
    cdim = conv.shape[1]
    ga0 = ga_col0 // tn
    gc0 = ga0 + d // tn
    return pl.pallas_call(
        _merge_kernel,
        grid=(d // tn, m // tm),
        in_specs=[pl.BlockSpec((tm, d), lambda j, i: (i, 0)),
                  pl.BlockSpec((tm, d), lambda j, i: (i, 0)),
                  pl.BlockSpec((tm, cdim), lambda j, i: (i, 0)),
                  pl.BlockSpec((None, d, tn), lambda j, i: (layer, 0, ga0 + j)),
                  pl.BlockSpec((None, d, tn), lambda j, i: (layer, 0, gc0 + j)),
                  pl.BlockSpec((None, d, tn), lambda j, i: (layer, 0, j)),
                  pl.BlockSpec((None, cdim, tn), lambda j, i: (layer, 0, j))],
        out_specs=pl.BlockSpec((tm, tn), lambda j, i: (i, j)),
        out_shape=jax.ShapeDtypeStruct((m, d), BF16),
        scratch_shapes=[pltpu.VMEM((d, tn), BF16), pltpu.VMEM((d, tn), BF16), pltpu.VMEM((d, tn), BF16),
                        pltpu.VMEM((cdim, tn), BF16)],
        compiler_params=_cparams(2),
        name="gate_merge",
    )(xn, attn, conv, w_in, w_in, w_attn_br, w_conv_br)


def _resid_norm_kernel(a_ref, w_ref, h_ref, g_ref, ho_ref, no_ref, wb_ref, *, rc):
    _cast_first(pl.program_id(0), w_ref, wb_ref)
    g = g_ref[...]
    for c in range(a_ref.shape[0] // rc):
        rows = pl.ds(c * rc, rc)
        h = h_ref[rows, :] + _dot(a_ref[rows, :], wb_ref[...])
        ho_ref[rows, :] = h
        no_ref[rows, :] = _rms(h, g).astype(no_ref.dtype)


def _resid_norm_call(a, w, layer, h, g, tm=512, rc=128):
    m, d = h.shape
    k = a.shape[1]
    row = lambda i: (i, 0)
    return pl.pallas_call(
        functools.partial(_resid_norm_kernel, rc=rc),
        grid=(m // tm,),
        in_specs=[pl.BlockSpec((tm, k), row),
                  pl.BlockSpec((None, k, d), lambda i: (layer, 0, 0), pipeline_mode=SINGLE),
                  pl.BlockSpec((tm, d), row),
                  pl.BlockSpec((1, d), lambda i: (0, 0))],
        out_specs=[pl.BlockSpec((tm, d), row), pl.BlockSpec((tm, d), row)],
        out_shape=[jax.ShapeDtypeStruct((m, d), F32), jax.ShapeDtypeStruct((m, d), BF16)],
        scratch_shapes=[pltpu.VMEM((k, d), BF16)],
        compiler_params=_cparams(1),
        name="mix_out_resid",
    )(a, w, h, g)


def _ffn_in_kernel(x_ref, wg_ref, wu_ref, cw_ref, cb_ref, o_ref, wgb_ref, wub_ref, pad_ref, *, rc):
    step = pl.program_id(1)
    _cast_first(step, wg_ref, wgb_ref)
    _cast_first(step, wu_ref, wub_ref)
    s = x_ref.shape[1]
    _zero_halo(pad_ref, s)

    def gate_rows(c):
        pad_ref[pl.ds(CONV_PAD + c * rc, rc), :] = _dot(x_ref[0, pl.ds(c * rc, rc), :], wgb_ref[...])

    w = cw_ref[...]
    bias = cb_ref[...]
    n_chunks = s // rc
    gate_rows(0)
    for c in range(n_chunks):
        if c + 1 < n_chunks:
            gate_rows(c + 1)
        rows = pl.ds(c * rc, rc)
        gate = _dwconv3_rows(pad_ref, c, rc, w) + bias
        up = _dot(x_ref[0, rows, :], wub_ref[...])
        o_ref[0, rows, :] = (gate * jax.nn.sigmoid(gate) * up).astype(o_ref.dtype)


def _ffn_in_call(hn, w_ffn_in, layer, conv_w, conv_b, *, d_ff, tn=512, rc=512):
    b, s, d = hn.shape
    nt = d_ff // tn
    return pl.pallas_call(
        functools.partial(_ffn_in_kernel, rc=rc),
        grid=(nt, b),
        in_specs=[pl.BlockSpec((1, s, d), lambda j, bi: (bi, 0, 0)),
                  pl.BlockSpec((None, d, tn), lambda j, bi: (layer, 0, j)),
                  pl.BlockSpec((None, d, tn), lambda j, bi: (layer, 0, nt + j)),
                  pl.BlockSpec((3, tn), lambda j, bi: (0, j)),
                  pl.BlockSpec((1, tn), lambda j, bi: (0, j))],
        out_specs=pl.BlockSpec((1, s, tn), lambda j, bi: (bi, 0, j)),
        out_shape=jax.ShapeDtypeStruct((b, s, d_ff), BF16),
        scratch_shapes=[pltpu.VMEM((d, tn), BF16), pltpu.VMEM((d, tn), BF16),
                        pltpu.VMEM((s + 2 * CONV_PAD, tn), F32)],
        compiler_params=_cparams(2),
        name="ffn_in",
    )(hn, w_ffn_in, w_ffn_in, conv_w, conv_b)


def _ffn_out_kernel(a_ref, w_ref, h_ref, ho_ref, wb_ref):
    _cast_first(pl.program_id(1), w_ref, wb_ref)
    ho_ref[...] = h_ref[...] + _dot(a_ref[...], wb_ref[...])


def _ffn_out_call(act, w, layer, h, tm=512, tn=512):
    m, d = h.shape
    k = act.shape[1]
    return pl.pallas_call(
        _ffn_out_kernel,
        grid=(d // tn, m // tm),
        in_specs=[pl.BlockSpec((tm, k), lambda j, i: (i, 0)),
                  pl.BlockSpec((None, k, tn), lambda j, i: (layer, 0, j)),
                  pl.BlockSpec((tm, tn), lambda j, i: (i, j))],
        out_specs=pl.BlockSpec((tm, tn), lambda j, i: (i, j)),
        out_shape=jax.ShapeDtypeStruct((m, d), F32),
        scratch_shapes=[pltpu.VMEM((k, tn), BF16)],
        compiler_params=_cparams(2),
        name="ffn_out_resid",
    )(act, w, h)


def _ple_kernel(h_ref, gp_ref, wg_ref, p_ref, wp_ref, gn_ref, *refs, last, rc):
    out_refs, (wgb_ref, wpb_ref) = refs[:-2], refs[-2:]
    _cast_first(pl.program_id(0), wg_ref, wgb_ref)
    _cast_first(pl.program_id(0), wp_ref, wpb_ref)
    gp = gp_ref[...]
    gn = gn_ref[...]
    for c in range(h_ref.shape[0] // rc):
        rows = pl.ds(c * rc, rc)
        h = h_ref[rows, :]
        gate = jax.nn.sigmoid(_dot(_rms(h, gp).astype(BF16), wgb_ref[...]))
        emb = _dot(p_ref[rows, :].astype(BF16), wpb_ref[...])
        h = h + gate * emb
        if last:
            out_refs[0][rows, :] = _rms(h, gn)
        else:
            out_refs[0][rows, :] = h
            out_refs[1][rows, :] = _rms(h, gn).astype(out_refs[1].dtype)


def _ple_call(h, g_ple, w_gate, p, layer, w_ple, g_next, *, last, tm=512, rc=128):
    m, d = h.shape
    pd = p.shape[-1]
    row = lambda i: (i, 0)
    const = lambda i: (0, 0)
    if last:
        out_specs = [pl.BlockSpec((tm, d), row)]
        out_shape = [jax.ShapeDtypeStruct((m, d), F32)]
    else:
        out_specs = [pl.BlockSpec((tm, d), row), pl.BlockSpec((tm, d), row)]
        out_shape = [jax.ShapeDtypeStruct((m, d), F32), jax.ShapeDtypeStruct((m, d), BF16)]
    return pl.pallas_call(
        functools.partial(_ple_kernel, last=last, rc=rc),
        grid=(m // tm,),
        in_specs=[pl.BlockSpec((tm, d), row),
                  pl.BlockSpec((1, d), const),
                  pl.BlockSpec((None, d, d), lambda i: (layer, 0, 0), pipeline_mode=SINGLE),
                  pl.BlockSpec((None, tm, pd), lambda i: (layer, i, 0)),
                  pl.BlockSpec((None, pd, d), lambda i: (layer, 0, 0), pipeline_mode=SINGLE),
                  pl.BlockSpec((1, d), const)],
        out_specs=out_specs,
        out_shape=out_shape,
        scratch_shapes=[pltpu.VMEM((d, d), BF16), pltpu.VMEM((pd, d), BF16)],
        compiler_params=_cparams(1),
        name="ple_final" if last else "ple",
    )(h, g_ple, w_gate, p, w_ple, g_next)


def _rope_tables(seq):
    axis_dim = HEAD_DIM // 2
    rows = seq // GRID_W
    row_ids = jnp.repeat(jnp.arange(rows, dtype=F32), GRID_W)
    col_ids = jnp.tile(jnp.arange(GRID_W, dtype=F32), rows)
    inv_freq = 1.0 / (ROPE_THETA ** (jnp.arange(0, axis_dim, 2, dtype=F32) / axis_dim))
    ang_r = row_ids[:, None] * inv_freq[None, :]
    ang_c = col_ids[:, None] * inv_freq[None, :]
    ang = jnp.concatenate([ang_r, ang_c, ang_r, ang_c], axis=-1)
    sign = jnp.where(jnp.arange(HEAD_DIM) < axis_dim, -1.0, 1.0).astype(F32)
    return jnp.cos(ang), jnp.sin(ang) * sign[None, :]


def kernel(x, p, norm_mix_g, w_in, q_norm_g, k_norm_g, w_attn_br, conv_mix_w, conv_mix_b, w_conv_br, w_mix_out, norm_ffn_g, w_ffn_in, conv_ffn_w, conv_ffn_b, w_ffn_out, norm_ple_g, w_ple_gate, w_ple, final_norm_g):
    b, s, d = x.shape
    depth = w_in.shape[0]
    m = b * s
    n_kv = d // HEAD_DIM // Q_PER_KV
    kv_dim = n_kv * HEAD_DIM
    conv_dim = conv_mix_w.shape[-1]
    d_ff = conv_ffn_w.shape[-1]
    conv_col0 = d + 2 * kv_dim
    gate_col0 = conv_col0 + 3 * conv_dim
    pd = p.shape[-1]

    cos, sin = _rope_tables(s)
    p2 = p.reshape(depth, m, pd)

    h = x.reshape(m, d)
    xn = _norm_call(h, norm_mix_g[0].reshape(1, d))
    out = None
    for i in range(depth):
        xn3 = xn.reshape(b, s, d)
        q = _q_call(xn3, w_in, i, cos, sin, q_norm_g[i].reshape(1, HEAD_DIM))
        k, v = _kv_call(xn3, w_in, i, cos, sin, k_norm_g[i].reshape(1, HEAD_DIM), k_col0=d, n_kv=n_kv)
        conv = _convbr_call(xn3, w_in, i, conv_mix_w[i], conv_mix_b[i].reshape(1, conv_dim),
                            col0=conv_col0, width=conv_dim)
        attn = _attn_call(q, k, v)
        merged = _merge_call(xn, attn.reshape(m, d), conv.reshape(m, conv_dim), w_in, i,
                             w_attn_br, w_conv_br, ga_col0=gate_col0)
        h, hn = _resid_norm_call(merged, w_mix_out, i, h, norm_ffn_g[i].reshape(1, d))
        act = _ffn_in_call(hn.reshape(b, s, d), w_ffn_in, i, conv_ffn_w[i],
                           conv_ffn_b[i].reshape(1, d_ff), d_ff=d_ff)
        h = _ffn_out_call(act.reshape(m, d_ff), w_ffn_out, i, h)
        last = i == depth - 1
        g_next = final_norm_g if last else norm_mix_g[i + 1]
        res = _ple_call(h, norm_ple_g[i].reshape(1, d), w_ple_gate, p2, i, w_ple, g_next.reshape(1, d),
                        last=last)
        if last:
            out = res[0]
        else:
            h, xn = res
    return out.reshape(b, s, d)
```

```python
import functools

import numpy as np
import jax
import jax.numpy as jnp
from jax import lax
from jax.experimental import pallas as pl
from jax.experimental.pallas import tpu as pltpu

HEAD_DIM = 128
Q_PER_KV = 4
ROPE_THETA = 10000.0
GRID_W = 64
EPS = 1e-6
LOG2E = 1.4426950408889634
CONV_PAD = 8
V7X_VMEM_LIMIT = 60 * 1024 * 1024

F32 = jnp.float32
BF16 = jnp.bfloat16
SINGLE = pl.Buffered(1)


def _cparams(n_axes):
    return pltpu.CompilerParams(dimension_semantics=("arbitrary",) * n_axes,
                                vmem_limit_bytes=V7X_VMEM_LIMIT)


def _rms(x, g):
    ms = jnp.mean(x * x, axis=-1, keepdims=True)
    return x * lax.rsqrt(ms + EPS) * g


def _dot(a, b):
    return jnp.dot(a, b, preferred_element_type=F32)


def _cast_first(step, w_ref, wb_ref):
    @pl.when(step == 0)
    def _():
        wb_ref[...] = w_ref[...].astype(wb_ref.dtype)


def _norm_kernel(x_ref, g_ref, o_ref):
    o_ref[...] = _rms(x_ref[...], g_ref[...]).astype(o_ref.dtype)


def _norm_call(x, g, tm=512):
    m, d = x.shape
    return pl.pallas_call(
        _norm_kernel,
        grid=(m // tm,),
        in_specs=[pl.BlockSpec((tm, d), lambda i: (i, 0)),
                  pl.BlockSpec((1, d), lambda i: (0, 0))],
        out_specs=pl.BlockSpec((tm, d), lambda i: (i, 0)),
        out_shape=jax.ShapeDtypeStruct((m, d), BF16),
        compiler_params=_cparams(1),
        name="init_norm",
    )(x, g)


def _swap_mid(x):
    lane = lax.broadcasted_iota(jnp.int32, x.shape, x.ndim - 1)
    seg = lane // (HEAD_DIM // 4)
    up = pltpu.roll(x, 3 * HEAD_DIM // 4, x.ndim - 1)
    dn = pltpu.roll(x, HEAD_DIM // 4, x.ndim - 1)
    return jnp.where(seg == 1, up, jnp.where(seg == 2, dn, x))


def _cast_swapped_first(step, w_ref, wb_ref):
    @pl.when(step == 0)
    def _():
        for hh in range(w_ref.shape[1] // HEAD_DIM):
            cols = slice(hh * HEAD_DIM, (hh + 1) * HEAD_DIM)
            wb_ref[:, cols] = _swap_mid(w_ref[:, cols]).astype(wb_ref.dtype)


def _rope_gain_tables(g_ref, scale, cos, sgn_sin):
    g = _swap_mid(jnp.broadcast_to(g_ref[...], (8, HEAD_DIM)))[0:1] * scale
    return cos * g, sgn_sin * pltpu.roll(jnp.broadcast_to(g, (8, HEAD_DIM)), HEAD_DIM // 2, 1)[0:1]


def _head_mean_matrix():
    r = lax.broadcasted_iota(jnp.int32, (2 * HEAD_DIM, 2 * HEAD_DIM), 0) // HEAD_DIM
    c = lax.broadcasted_iota(jnp.int32, (2 * HEAD_DIM, 2 * HEAD_DIM), 1) // HEAD_DIM
    return jnp.where(r == c, 1.0 / HEAD_DIM, 0.0).astype(BF16)


def _norm_rope_heads(z, ca, sb, bd, o_ref, rows):
    z2 = (z * z).astype(BF16)
    for pair in range(z.shape[1] // (2 * HEAD_DIM)):
        rs = lax.rsqrt(_dot(z2[:, pair * 2 * HEAD_DIM:(pair + 1) * 2 * HEAD_DIM], bd) + EPS)
        for sub in range(2):
            hh = 2 * pair + sub
            x = z[:, hh * HEAD_DIM:(hh + 1) * HEAD_DIM]
            y = (x * ca + pltpu.roll(x, HEAD_DIM // 2, 1) * sb) * rs[:, sub * HEAD_DIM:(sub + 1) * HEAD_DIM]
            o_ref[0, hh, rows, :] = y.astype(o_ref.dtype)


def _q_kernel(x_ref, w_ref, cos_ref, sin_ref, g_ref, o_ref, wb_ref, *, rc):
    _cast_swapped_first(pl.program_id(1), w_ref, wb_ref)
    s = x_ref.shape[1]
    scale = LOG2E / float(np.sqrt(HEAD_DIM))
    bd = _head_mean_matrix()
    for c in range(s // rc):
        rows = pl.ds(c * rc, rc)
        ca, sb = _rope_gain_tables(g_ref, scale, cos_ref[rows, :], sin_ref[rows, :])
        _norm_rope_heads(_dot(x_ref[0, rows, :], wb_ref[...]), ca, sb, bd, o_ref, rows)


def _q_call(xn, w_in, layer, cos, sin, qg, *, tn=1024, rc=256):
    b, s, d = xn.shape
    hpt = tn // HEAD_DIM
    const = lambda j, bi: (0, 0)
    return pl.pallas_call(
        functools.partial(_q_kernel, rc=rc),
        grid=(d // tn, b),
        in_specs=[pl.BlockSpec((1, s, d), lambda j, bi: (bi, 0, 0)),
                  pl.BlockSpec((None, d, tn), lambda j, bi: (layer, 0, j)),
                  pl.BlockSpec((s, HEAD_DIM), const),
                  pl.BlockSpec((s, HEAD_DIM), const),
                  pl.BlockSpec((1, HEAD_DIM), const)],
        out_specs=pl.BlockSpec((1, hpt, s, HEAD_DIM), lambda j, bi: (bi, j, 0, 0)),
        out_shape=jax.ShapeDtypeStruct((b, d // HEAD_DIM, s, HEAD_DIM), BF16),
        scratch_shapes=[pltpu.VMEM((d, tn), BF16)],
        compiler_params=_cparams(2),
        name="q_proj",
    )(xn, w_in, cos, sin, qg)


def _kv_kernel(x_ref, wk_ref, wv_ref, cos_ref, sin_ref, g_ref, k_ref, v_ref, wkb_ref, wvb_ref, *, rc):
    _cast_swapped_first(pl.program_id(0), wk_ref, wkb_ref)
    _cast_first(pl.program_id(0), wv_ref, wvb_ref)
    s = x_ref.shape[1]
    n_kv = k_ref.shape[1]
    ones = jnp.ones((rc, HEAD_DIM), v_ref.dtype)
    bd = _head_mean_matrix()
    for c in range(s // rc):
        rows = pl.ds(c * rc, rc)
        x = x_ref[0, rows, :]
        ca, sb = _rope_gain_tables(g_ref, 1.0, cos_ref[rows, :], sin_ref[rows, :])
        _norm_rope_heads(_dot(x, wkb_ref[...]), ca, sb, bd, k_ref, rows)
        zv = _dot(x, wvb_ref[...])
        for hh in range(n_kv):
            v_ref[0, hh, rows, :HEAD_DIM] = zv[:, hh * HEAD_DIM:(hh + 1) * HEAD_DIM].astype(v_ref.dtype)
            v_ref[0, hh, rows, HEAD_DIM:] = ones


def _kv_call(xn, w_in, layer, cos, sin, kg, *, k_col0, n_kv, rc=256):
    b, s, d = xn.shape
    kv = n_kv * HEAD_DIM
    kt = k_col0 // kv
    const = lambda bi: (0, 0)
    return pl.pallas_call(
        functools.partial(_kv_kernel, rc=rc),
        grid=(b,),
        in_specs=[pl.BlockSpec((1, s, d), lambda bi: (bi, 0, 0)),
                  pl.BlockSpec((None, d, kv), lambda bi: (layer, 0, kt), pipeline_mode=SINGLE),
                  pl.BlockSpec((None, d, kv), lambda bi: (layer, 0, kt + 1), pipeline_mode=SINGLE),
                  pl.BlockSpec((s, HEAD_DIM), const),
                  pl.BlockSpec((s, HEAD_DIM), const),
                  pl.BlockSpec((1, HEAD_DIM), const)],
        out_specs=[pl.BlockSpec((1, n_kv, s, HEAD_DIM), lambda bi: (bi, 0, 0, 0)),
                   pl.BlockSpec((1, n_kv, s, 2 * HEAD_DIM), lambda bi: (bi, 0, 0, 0))],
        out_shape=[jax.ShapeDtypeStruct((b, n_kv, s, HEAD_DIM), BF16),
                   jax.ShapeDtypeStruct((b, n_kv, s, 2 * HEAD_DIM), BF16)],
        scratch_shapes=[pltpu.VMEM((d, kv), BF16), pltpu.VMEM((d, kv), BF16)],
        compiler_params=_cparams(1),
        name="kv_proj",
    )(xn, w_in, w_in, cos, sin, kg)


def _dwconv3_rows(pad_ref, c, rc, w):
    base = CONV_PAD + c * rc
    prev = pad_ref[pl.ds(base - 1, rc), :]
    cur = pad_ref[pl.ds(base, rc), :]
    nxt = pad_ref[pl.ds(base + 1, rc), :]
    return prev * w[0:1, :] + cur * w[1:2, :] + nxt * w[2:3, :]


def _zero_halo(pad_ref, s):
    zeros = jnp.zeros((CONV_PAD, pad_ref.shape[1]), pad_ref.dtype)
    pad_ref[pl.ds(0, CONV_PAD), :] = zeros
    pad_ref[pl.ds(CONV_PAD + s, CONV_PAD), :] = zeros


def _convbr_kernel(x_ref, wb_ref, wc_ref, wu_ref, cw_ref, cb_ref, o_ref, wbb_ref, wcb_ref, wub_ref, pad_ref,
                   *, rc):
    step = pl.program_id(1)
    _cast_first(step, wb_ref, wbb_ref)
    _cast_first(step, wc_ref, wcb_ref)
    _cast_first(step, wu_ref, wub_ref)
    s = x_ref.shape[1]
    _zero_halo(pad_ref, s)
    for c in range(s // rc):
        x = x_ref[0, pl.ds(c * rc, rc), :]
        pad_ref[pl.ds(CONV_PAD + c * rc, rc), :] = _dot(x, wcb_ref[...]) * _dot(x, wub_ref[...])
    w = cw_ref[...]
    bias = cb_ref[...]
    for c in range(s // rc):
        rows = pl.ds(c * rc, rc)
        conv = _dwconv3_rows(pad_ref, c, rc, w) + bias
        bg = _dot(x_ref[0, rows, :], wbb_ref[...])
        o_ref[0, rows, :] = (bg * conv).astype(o_ref.dtype)


def _convbr_call(xn, w_in, layer, conv_w, conv_b, *, col0, width, tn=256, rc=512):
    b, s, d = xn.shape
    t0 = col0 // tn
    nt = width // tn
    wspec = lambda off: pl.BlockSpec((None, d, tn), lambda j, bi: (layer, 0, t0 + off * nt + j))
    return pl.pallas_call(
        functools.partial(_convbr_kernel, rc=rc),
        grid=(nt, b),
        in_specs=[pl.BlockSpec((1, s, d), lambda j, bi: (bi, 0, 0)),
                  wspec(0), wspec(1), wspec(2),
                  pl.BlockSpec((3, tn), lambda j, bi: (0, j)),
                  pl.BlockSpec((1, tn), lambda j, bi: (0, j))],
        out_specs=pl.BlockSpec((1, s, tn), lambda j, bi: (bi, 0, j)),
        out_shape=jax.ShapeDtypeStruct((b, s, width), BF16),
        scratch_shapes=[pltpu.VMEM((d, tn), BF16), pltpu.VMEM((d, tn), BF16), pltpu.VMEM((d, tn), BF16),
                        pltpu.VMEM((s + 2 * CONV_PAD, tn), F32)],
        compiler_params=_cparams(2),
        name="conv_branch",
    )(xn, w_in, w_in, w_in, conv_w, conv_b)


def _attn_kernel(q_ref, k_ref, v_ref, o_ref, *, rc):
    k = k_ref[0, 0]
    v = v_ref[0, 0]
    tq = q_ref.shape[2]
    for g in range(Q_PER_KV):
        for r in range(tq // rc):
            rows = pl.ds(r * rc, rc)
            sc = lax.dot_general(q_ref[0, g, rows, :], k, (((1,), (1,)), ((), ())),
                                 preferred_element_type=F32)
            m = jnp.max(sc, axis=-1, keepdims=True)
            p = jnp.exp2(sc - m).astype(BF16)
            oe = _dot(p, v)
            o = oe[:, :HEAD_DIM] / oe[:, HEAD_DIM:]
            o_ref[0, rows, g * HEAD_DIM:(g + 1) * HEAD_DIM] = o.astype(o_ref.dtype)


def _attn_call(q, k, v, *, tq=2048, rc=256):
    b, n_q, s, _ = q.shape
    n_kv = k.shape[1]
    gw = Q_PER_KV * HEAD_DIM
    return pl.pallas_call(
        functools.partial(_attn_kernel, rc=rc),
        grid=(b, n_kv, s // tq),
        in_specs=[pl.BlockSpec((1, Q_PER_KV, tq, HEAD_DIM), lambda bi, h, i: (bi, h, i, 0)),
                  pl.BlockSpec((1, 1, s, HEAD_DIM), lambda bi, h, i: (bi, h, 0, 0)),
                  pl.BlockSpec((1, 1, s, 2 * HEAD_DIM), lambda bi, h, i: (bi, h, 0, 0))],
        out_specs=pl.BlockSpec((1, tq, gw), lambda bi, h, i: (bi, i, h)),
        out_shape=jax.ShapeDtypeStruct((b, s, n_q * HEAD_DIM), BF16),
        compiler_params=_cparams(3),
        name="gqa_attention",
    )(q, k, v)


def _merge_kernel(x_ref, a_ref, c_ref, wga_ref, wgc_ref, wa_ref, wc_ref, o_ref,
                  wgab_ref, wgcb_ref, wab_ref, wcb_ref):
    step = pl.program_id(1)
    _cast_first(step, wga_ref, wgab_ref)
    _cast_first(step, wgc_ref, wgcb_ref)
    _cast_first(step, wa_ref, wab_ref)
    _cast_first(step, wc_ref, wcb_ref)
    x = x_ref[...]
    ga = jax.nn.sigmoid(_dot(x, wgab_ref[...]))
    gc = jax.nn.sigmoid(_dot(x, wgcb_ref[...]))
    merged = ga * _dot(a_ref[...], wab_ref[...]) + gc * _dot(c_ref[...], wcb_ref[...])
    o_ref[...] = merged.astype(o_ref.dtype)


def _merge_call(xn, attn, conv, w_in, layer, w_attn_br, w_conv_br, *, ga_col0, tm=512, tn=512):
    m, d = xn.shape
    cdim = conv.shape[1]
    ga0 = ga_col0 // tn
    gc0 = ga0 + d // tn
    return pl.pallas_call(
        _merge_kernel,
        grid=(d // tn, m // tm),
        in_specs=[pl.BlockSpec((tm, d), lambda j, i: (i, 0)),
                  pl.BlockSpec((tm, d), lambda j, i: (i, 0)),
                  pl.BlockSpec((tm, cdim), lambda j, i: (i, 0)),
                  pl.BlockSpec((None, d, tn), lambda j, i: (layer, 0, ga0 + j)),
                  pl.BlockSpec((None, d, tn), lambda j, i: (layer, 0, gc0 + j)),
                  pl.BlockSpec((None, d, tn), lambda j, i: (layer, 0, j)),
                  pl.BlockSpec((None, cdim, tn), lambda j, i: (layer, 0, j))],
        out_specs=pl.BlockSpec((tm, tn), lambda j, i: (i, j)),
        out_shape=jax.ShapeDtypeStruct((m, d), BF16),
        scratch_shapes=[pltpu.VMEM((d, tn), BF16), pltpu.VMEM((d, tn), BF16), pltpu.VMEM((d, tn), BF16),
                        pltpu.VMEM((cdim, tn), BF16)],
        compiler_params=_cparams(2),
        name="gate_merge",
    )(xn, attn, conv, w_in, w_in, w_attn_br, w_conv_br)


def _resid_norm_kernel(a_ref, w_ref, h_ref, g_ref, ho_ref, no_ref, wb_ref, *, rc):
    _cast_first(pl.program_id(0), w_ref, wb_ref)
    g = g_ref[...]
    for c in range(a_ref.shape[0] // rc):
        rows = pl.ds(c * rc, rc)
        h = h_ref[rows, :] + _dot(a_ref[rows, :], wb_ref[...])
        ho_ref[rows, :] = h
        no_ref[rows, :] = _rms(h, g).astype(no_ref.dtype)


def _resid_norm_call(a, w, layer, h, g, tm=512, rc=256):
    m, d = h.shape
    k = a.shape[1]
    row = lambda i: (i, 0)
    return pl.pallas_call(
        functools.partial(_resid_norm_kernel, rc=rc),
        grid=(m // tm,),
        in_specs=[pl.BlockSpec((tm, k), row),
                  pl.BlockSpec((None, k, d), lambda i: (layer, 0, 0), pipeline_mode=SINGLE),
                  pl.BlockSpec((tm, d), row),
                  pl.BlockSpec((1, d), lambda i: (0, 0))],
        out_specs=[pl.BlockSpec((tm, d), row), pl.BlockSpec((tm, d), row)],
        out_shape=[jax.ShapeDtypeStruct((m, d), F32), jax.ShapeDtypeStruct((m, d), BF16)],
        scratch_shapes=[pltpu.VMEM((k, d), BF16)],
        compiler_params=_cparams(1),
        name="mix_out_resid",
    )(a, w, h, g)


def _ffn_in_kernel(x_ref, wg_ref, wu_ref, cw_ref, cb_ref, o_ref, wgb_ref, wub_ref, pad_ref, *, rc):
    step = pl.program_id(1)
    _cast_first(step, wg_ref, wgb_ref)
    _cast_first(step, wu_ref, wub_ref)
    s = x_ref.shape[1]
    _zero_halo(pad_ref, s)
    for c in range(s // rc):
        pad_ref[pl.ds(CONV_PAD + c * rc, rc), :] = _dot(x_ref[0, pl.ds(c * rc, rc), :], wgb_ref[...])
    w = cw_ref[...]
    bias = cb_ref[...]
    for c in range(s // rc):
        rows = pl.ds(c * rc, rc)
        gate = _dwconv3_rows(pad_ref, c, rc, w) + bias
        up = _dot(x_ref[0, rows, :], wub_ref[...])
        o_ref[0, rows, :] = (gate * jax.nn.sigmoid(gate) * up).astype(o_ref.dtype)


def _ffn_in_call(hn, w_ffn_in, layer, conv_w, conv_b, *, d_ff, tn=512, rc=512):
    b, s, d = hn.shape
    nt = d_ff // tn
    return pl.pallas_call(
        functools.partial(_ffn_in_kernel, rc=rc),
        grid=(nt, b),
        in_specs=[pl.BlockSpec((1, s, d), lambda j, bi: (bi, 0, 0)),
                  pl.BlockSpec((None, d, tn), lambda j, bi: (layer, 0, j)),
                  pl.BlockSpec((None, d, tn), lambda j, bi: (layer, 0, nt + j)),
                  pl.BlockSpec((3, tn), lambda j, bi: (0, j)),
                  pl.BlockSpec((1, tn), lambda j, bi: (0, j))],
        out_specs=pl.BlockSpec((1, s, tn), lambda j, bi: (bi, 0, j)),
        out_shape=jax.ShapeDtypeStruct((b, s, d_ff), BF16),
        scratch_shapes=[pltpu.VMEM((d, tn), BF16), pltpu.VMEM((d, tn), BF16),
                        pltpu.VMEM((s + 2 * CONV_PAD, tn), F32)],
        compiler_params=_cparams(2),
        name="ffn_in",
    )(hn, w_ffn_in, w_ffn_in, conv_w, conv_b)


def _ffn_out_kernel(a_ref, w_ref, h_ref, ho_ref, wb_ref):
    _cast_first(pl.program_id(1), w_ref, wb_ref)
    ho_ref[...] = h_ref[...] + _dot(a_ref[...], wb_ref[...])


def _ffn_out_call(act, w, layer, h, tm=512, tn=1024):
    m, d = h.shape
    k = act.shape[1]
    return pl.pallas_call(
        _ffn_out_kernel,
        grid=(d // tn, m // tm),
        in_specs=[pl.BlockSpec((tm, k), lambda j, i: (i, 0)),
                  pl.BlockSpec((None, k, tn), lambda j, i: (layer, 0, j), pipeline_mode=SINGLE),
                  pl.BlockSpec((tm, tn), lambda j, i: (i, j))],
        out_specs=pl.BlockSpec((tm, tn), lambda j, i: (i, j)),
        out_shape=jax.ShapeDtypeStruct((m, d), F32),
        scratch_shapes=[pltpu.VMEM((k, tn), BF16)],
        compiler_params=_cparams(2),
        name="ffn_out_resid",
    )(act, w, h)


def _ple_kernel(h_ref, gp_ref, wg_ref, p_ref, wp_ref, gn_ref, *refs, last, rc):
    out_refs, (wgb_ref, wpb_ref) = refs[:-2], refs[-2:]
    _cast_first(pl.program_id(0), wg_ref, wgb_ref)
    _cast_first(pl.program_id(0), wp_ref, wpb_ref)
    gp = gp_ref[...]
    gn = gn_ref[...]
    for c in range(h_ref.shape[0] // rc):
        rows = pl.ds(c * rc, rc)
        h = h_ref[rows, :]
        gate = jax.nn.sigmoid(_dot(_rms(h, gp).astype(BF16), wgb_ref[...]))
        emb = _dot(p_ref[rows, :].astype(BF16), wpb_ref[...])
        h = h + gate * emb
        if last:
            out_refs[0][rows, :] = _rms(h, gn)
        else:
            out_refs[0][rows, :] = h
            out_refs[1][rows, :] = _rms(h, gn).astype(out_refs[1].dtype)


def _ple_call(h, g_ple, w_gate, p, layer, w_ple, g_next, *, last, tm=512, rc=128):
    m, d = h.shape
    pd = p.shape[-1]
    row = lambda i: (i, 0)
    const = lambda i: (0, 0)
    if last:
        out_specs = [pl.BlockSpec((tm, d), row)]
        out_shape = [jax.ShapeDtypeStruct((m, d), F32)]
    else:
        out_specs = [pl.BlockSpec((tm, d), row), pl.BlockSpec((tm, d), row)]
        out_shape = [jax.ShapeDtypeStruct((m, d), F32), jax.ShapeDtypeStruct((m, d), BF16)]
    return pl.pallas_call(
        functools.partial(_ple_kernel, last=last, rc=rc),
        grid=(m // tm,),
        in_specs=[pl.BlockSpec((tm, d), row),
                  pl.BlockSpec((1, d), const),
                  pl.BlockSpec((None, d, d), lambda i: (layer, 0, 0), pipeline_mode=SINGLE),
                  pl.BlockSpec((None, tm, pd), lambda i: (layer, i, 0)),
                  pl.BlockSpec((None, pd, d), lambda i: (layer, 0, 0), pipeline_mode=SINGLE),
                  pl.BlockSpec((1, d), const)],
        out_specs=out_specs,
        out_shape=out_shape,
        scratch_shapes=[pltpu.VMEM((d, d), BF16), pltpu.VMEM((pd, d), BF16)],
        compiler_params=_cparams(1),
        name="ple_final" if last else "ple",
    )(h, g_ple, w_gate, p, w_ple, g_next)


def _rope_tables(seq):
    axis_dim = HEAD_DIM // 2
    rows = seq // GRID_W
    row_ids = jnp.repeat(jnp.arange(rows, dtype=F32), GRID_W)
    col_ids = jnp.tile(jnp.arange(GRID_W, dtype=F32), rows)
    inv_freq = 1.0 / (ROPE_THETA ** (jnp.arange(0, axis_dim, 2, dtype=F32) / axis_dim))
    ang_r = row_ids[:, None] * inv_freq[None, :]
    ang_c = col_ids[:, None] * inv_freq[None, :]
    ang = jnp.concatenate([ang_r, ang_c, ang_r, ang_c], axis=-1)
    sign = jnp.where(jnp.arange(HEAD_DIM) < axis_dim, -1.0, 1.0).astype(F32)
    return jnp.cos(ang), jnp.sin(ang) * sign[None, :]


def kernel(x, p, norm_mix_g, w_in, q_norm_g, k_norm_g, w_attn_br, conv_mix_w, conv_mix_b, w_conv_br, w_mix_out, norm_ffn_g, w_ffn_in, conv_ffn_w, conv_ffn_b, w_ffn_out, norm_ple_g, w_ple_gate, w_ple, final_norm_g):
    b, s, d = x.shape
    depth = w_in.shape[0]
    m = b * s
    n_kv = d // HEAD_DIM // Q_PER_KV
    kv_dim = n_kv * HEAD_DIM
    conv_dim = conv_mix_w.shape[-1]
    d_ff = conv_ffn_w.shape[-1]
    conv_col0 = d + 2 * kv_dim
    gate_col0 = conv_col0 + 3 * conv_dim
    pd = p.shape[-1]

    cos, sin = _rope_tables(s)
    p2 = p.reshape(depth, m, pd)

    h = x.reshape(m, d)
    xn = _norm_call(h, norm_mix_g[0].reshape(1, d))
    out = None
    for i in range(depth):
        xn3 = xn.reshape(b, s, d)
        q = _q_call(xn3, w_in, i, cos, sin, q_norm_g[i].reshape(1, HEAD_DIM))
        k, v = _kv_call(xn3, w_in, i, cos, sin, k_norm_g[i].reshape(1, HEAD_DIM), k_col0=d, n_kv=n_kv)
        conv = _convbr_call(xn3, w_in, i, conv_mix_w[i], conv_mix_b[i].reshape(1, conv_dim),
                            col0=conv_col0, width=conv_dim)
        attn = _attn_call(q, k, v)
        merged = _merge_call(xn, attn.reshape(m, d), conv.reshape(m, conv_dim), w_in, i,
                             w_attn_br, w_conv_br, ga_col0=gate_col0)
        h, hn = _resid_norm_call(merged, w_mix_out, i, h, norm_ffn_g[i].reshape(1, d))
        act = _ffn_in_call(hn.reshape(b, s, d), w_ffn_in, i, conv_ffn_w[i],
                           conv_ffn_b[i].reshape(1, d_ff), d_ff=d_ff)
        h = _ffn_out_call(act.reshape(m, d_ff), w_ffn_out, i, h)
        last = i == depth - 1
        g_next = final_norm_g if last else norm_mix_g[i + 1]
        res = _ple_call(h, norm_ple_g[i].reshape(1, d), w_ple_gate, p2, i, w_ple, g_next.reshape(1, d),
                        last=last)
        if last:
            out = res[0]
        else:
            h, xn = res
    return out.reshape(b, s, d)
```

```python
import functools

import numpy as np
import jax
import jax.numpy as jnp
from jax import lax
from jax.experimental import pallas as pl
from jax.experimental.pallas import tpu as pltpu

HEAD_DIM = 128
Q_PER_KV = 4
ROPE_THETA = 10000.0
GRID_W = 64
EPS = 1e-6
LOG2E = 1.4426950408889634
CONV_PAD = 8
V7X_VMEM_LIMIT = 60 * 1024 * 1024

F32 = jnp.float32
BF16 = jnp.bfloat16
SINGLE = pl.Buffered(1)


def _cparams(n_axes):
    return pltpu.CompilerParams(dimension_semantics=("arbitrary",) * n_axes,
                                vmem_limit_bytes=V7X_VMEM_LIMIT)


def _rms(x, g):
    ms = jnp.mean(x * x, axis=-1, keepdims=True)
    return x * lax.rsqrt(ms + EPS) * g


def _dot(a, b):
    return jnp.dot(a, b, preferred_element_type=F32)


def _cast_first(step, w_ref, wb_ref):
    @pl.when(step == 0)
    def _():
        wb_ref[...] = w_ref[...].astype(wb_ref.dtype)


def _norm_kernel(x_ref, g_ref, o_ref):
    o_ref[...] = _rms(x_ref[...], g_ref[...]).astype(o_ref.dtype)


def _norm_call(x, g, tm=512):
    m, d = x.shape
    return pl.pallas_call(
        _norm_kernel,
        grid=(m // tm,),
        in_specs=[pl.BlockSpec((tm, d), lambda i: (i, 0)),
                  pl.BlockSpec((1, d), lambda i: (0, 0))],
        out_specs=pl.BlockSpec((tm, d), lambda i: (i, 0)),
        out_shape=jax.ShapeDtypeStruct((m, d), BF16),
        compiler_params=_cparams(1),
        name="init_norm",
    )(x, g)


def _swap_mid(x):
    lane = lax.broadcasted_iota(jnp.int32, x.shape, x.ndim - 1)
    seg = lane // (HEAD_DIM // 4)
    up = pltpu.roll(x, 3 * HEAD_DIM // 4, x.ndim - 1)
    dn = pltpu.roll(x, HEAD_DIM // 4, x.ndim - 1)
    return jnp.where(seg == 1, up, jnp.where(seg == 2, dn, x))


def _cast_swapped_first(step, w_ref, wb_ref):
    @pl.when(step == 0)
    def _():
        for hh in range(w_ref.shape[1] // HEAD_DIM):
            cols = slice(hh * HEAD_DIM, (hh + 1) * HEAD_DIM)
            wb_ref[:, cols] = _swap_mid(w_ref[:, cols]).astype(wb_ref.dtype)


def _rope_gain_tables(g_ref, scale, cos, sgn_sin):
    g = _swap_mid(jnp.broadcast_to(g_ref[...], (8, HEAD_DIM)))[0:1] * scale
    return cos * g, sgn_sin * pltpu.roll(jnp.broadcast_to(g, (8, HEAD_DIM)), HEAD_DIM // 2, 1)[0:1]


def _head_mean_matrix():
    r = lax.broadcasted_iota(jnp.int32, (2 * HEAD_DIM, 2 * HEAD_DIM), 0) // HEAD_DIM
    c = lax.broadcasted_iota(jnp.int32, (2 * HEAD_DIM, 2 * HEAD_DIM), 1) // HEAD_DIM
    return jnp.where(r == c, 1.0 / HEAD_DIM, 0.0).astype(BF16)


def _norm_rope_heads(z, ca, sb, bd, o_ref, rows):
    z2 = (z * z).astype(BF16)
    for pair in range(z.shape[1] // (2 * HEAD_DIM)):
        rs = lax.rsqrt(_dot(z2[:, pair * 2 * HEAD_DIM:(pair + 1) * 2 * HEAD_DIM], bd) + EPS)
        for sub in range(2):
            hh = 2 * pair + sub
            x = z[:, hh * HEAD_DIM:(hh + 1) * HEAD_DIM]
            y = (x * ca + pltpu.roll(x, HEAD_DIM // 2, 1) * sb) * rs[:, sub * HEAD_DIM:(sub + 1) * HEAD_DIM]
            o_ref[0, hh, rows, :] = y.astype(o_ref.dtype)


def _q_kernel(x_ref, w_ref, cos_ref, sin_ref, g_ref, o_ref, wb_ref, *, rc):
    _cast_swapped_first(pl.program_id(1), w_ref, wb_ref)
    s = x_ref.shape[1]
    scale = LOG2E / float(np.sqrt(HEAD_DIM))
    bd = _head_mean_matrix()
    for c in range(s // rc):
        rows = pl.ds(c * rc, rc)
        ca, sb = _rope_gain_tables(g_ref, scale, cos_ref[rows, :], sin_ref[rows, :])
        _norm_rope_heads(_dot(x_ref[0, rows, :], wb_ref[...]), ca, sb, bd, o_ref, rows)


def _q_call(xn, w_in, layer, cos, sin, qg, *, tn=1024, rc=256):
    b, s, d = xn.shape
    hpt = tn // HEAD_DIM
    const = lambda j, bi: (0, 0)
    return pl.pallas_call(
        functools.partial(_q_kernel, rc=rc),
        grid=(d // tn, b),
        in_specs=[pl.BlockSpec((1, s, d), lambda j, bi: (bi, 0, 0)),
                  pl.BlockSpec((None, d, tn), lambda j, bi: (layer, 0, j)),
                  pl.BlockSpec((s, HEAD_DIM), const),
                  pl.BlockSpec((s, HEAD_DIM), const),
                  pl.BlockSpec((1, HEAD_DIM), const)],
        out_specs=pl.BlockSpec((1, hpt, s, HEAD_DIM), lambda j, bi: (bi, j, 0, 0)),
        out_shape=jax.ShapeDtypeStruct((b, d // HEAD_DIM, s, HEAD_DIM), BF16),
        scratch_shapes=[pltpu.VMEM((d, tn), BF16)],
        compiler_params=_cparams(2),
        name="q_proj",
    )(xn, w_in, cos, sin, qg)


def _kv_kernel(x_ref, wk_ref, wv_ref, cos_ref, sin_ref, g_ref, k_ref, v_ref, wkb_ref, wvb_ref, *, rc):
    _cast_swapped_first(pl.program_id(0), wk_ref, wkb_ref)
    _cast_first(pl.program_id(0), wv_ref, wvb_ref)
    s = x_ref.shape[1]
    n_kv = k_ref.shape[1]
    bd = _head_mean_matrix()
    for c in range(s // rc):
        rows = pl.ds(c * rc, rc)
        x = x_ref[0, rows, :]
        ca, sb = _rope_gain_tables(g_ref, 1.0, cos_ref[rows, :], sin_ref[rows, :])
        _norm_rope_heads(_dot(x, wkb_ref[...]), ca, sb, bd, k_ref, rows)
        zv = _dot(x, wvb_ref[...])
        for hh in range(n_kv):
            v_ref[0, hh, rows, :] = zv[:, hh * HEAD_DIM:(hh + 1) * HEAD_DIM].astype(v_ref.dtype)


def _kv_call(xn, w_in, layer, cos, sin, kg, *, k_col0, n_kv, rc=512):
    b, s, d = xn.shape
    kv = n_kv * HEAD_DIM
    kt = k_col0 // kv
    const = lambda bi: (0, 0)
    return pl.pallas_call(
        functools.partial(_kv_kernel, rc=rc),
        grid=(b,),
        in_specs=[pl.BlockSpec((1, s, d), lambda bi: (bi, 0, 0)),
                  pl.BlockSpec((None, d, kv), lambda bi: (layer, 0, kt), pipeline_mode=SINGLE),
                  pl.BlockSpec((None, d, kv), lambda bi: (layer, 0, kt + 1), pipeline_mode=SINGLE),
                  pl.BlockSpec((s, HEAD_DIM), const),
                  pl.BlockSpec((s, HEAD_DIM), const),
                  pl.BlockSpec((1, HEAD_DIM), const)],
        out_specs=[pl.BlockSpec((1, n_kv, s, HEAD_DIM), lambda bi: (bi, 0, 0, 0)),
                   pl.BlockSpec((1, n_kv, s, HEAD_DIM), lambda bi: (bi, 0, 0, 0))],
        out_shape=[jax.ShapeDtypeStruct((b, n_kv, s, HEAD_DIM), BF16),
                   jax.ShapeDtypeStruct((b, n_kv, s, HEAD_DIM), BF16)],
        scratch_shapes=[pltpu.VMEM((d, kv), BF16), pltpu.VMEM((d, kv), BF16)],
        compiler_params=_cparams(1),
        name="kv_proj",
    )(xn, w_in, w_in, cos, sin, kg)


def _dwconv3_rows(pad_ref, c, rc, w):
    base = CONV_PAD + c * rc
    prev = pad_ref[pl.ds(base - 1, rc), :]
    cur = pad_ref[pl.ds(base, rc), :]
    nxt = pad_ref[pl.ds(base + 1, rc), :]
    return prev * w[0:1, :] + cur * w[1:2, :] + nxt * w[2:3, :]


def _zero_halo(pad_ref, s):
    zeros = jnp.zeros((CONV_PAD, pad_ref.shape[1]), pad_ref.dtype)
    pad_ref[pl.ds(0, CONV_PAD), :] = zeros
    pad_ref[pl.ds(CONV_PAD + s, CONV_PAD), :] = zeros


def _convbr_kernel(x_ref, wb_ref, wc_ref, wu_ref, cw_ref, cb_ref, o_ref, wbb_ref, wcb_ref, wub_ref, pad_ref,
                   *, rc):
    step = pl.program_id(1)
    _cast_first(step, wb_ref, wbb_ref)
    _cast_first(step, wc_ref, wcb_ref)
    _cast_first(step, wu_ref, wub_ref)
    s = x_ref.shape[1]
    _zero_halo(pad_ref, s)
    for c in range(s // rc):
        x = x_ref[0, pl.ds(c * rc, rc), :]
        pad_ref[pl.ds(CONV_PAD + c * rc, rc), :] = _dot(x, wcb_ref[...]) * _dot(x, wub_ref[...])
    w = cw_ref[...]
    bias = cb_ref[...]
    for c in range(s // rc):
        rows = pl.ds(c * rc, rc)
        conv = _dwconv3_rows(pad_ref, c, rc, w) + bias
        bg = _dot(x_ref[0, rows, :], wbb_ref[...])
        o_ref[0, rows, :] = (bg * conv).astype(o_ref.dtype)


def _convbr_call(xn, w_in, layer, conv_w, conv_b, *, col0, width, tn=256, rc=512):
    b, s, d = xn.shape
    t0 = col0 // tn
    nt = width // tn
    wspec = lambda off: pl.BlockSpec((None, d, tn), lambda j, bi: (layer, 0, t0 + off * nt + j))
    return pl.pallas_call(
        functools.partial(_convbr_kernel, rc=rc),
        grid=(nt, b),
        in_specs=[pl.BlockSpec((1, s, d), lambda j, bi: (bi, 0, 0)),
                  wspec(0), wspec(1), wspec(2),
                  pl.BlockSpec((3, tn), lambda j, bi: (0, j)),
                  pl.BlockSpec((1, tn), lambda j, bi: (0, j))],
        out_specs=pl.BlockSpec((1, s, tn), lambda j, bi: (bi, 0, j)),
        out_shape=jax.ShapeDtypeStruct((b, s, width), BF16),
        scratch_shapes=[pltpu.VMEM((d, tn), BF16), pltpu.VMEM((d, tn), BF16), pltpu.VMEM((d, tn), BF16),
                        pltpu.VMEM((s + 2 * CONV_PAD, tn), F32)],
        compiler_params=_cparams(2),
        name="conv_branch",
    )(xn, w_in, w_in, w_in, conv_w, conv_b)


def _attn_kernel(q_ref, k_ref, v_ref, o_ref, ve_ref, *, rc):
    s = k_ref.shape[2]

    @pl.when(pl.program_id(2) == 0)
    def _():
        ve_ref[:, :HEAD_DIM] = v_ref[0, 0]
        ve_ref[:, HEAD_DIM:] = jnp.ones((s, HEAD_DIM), ve_ref.dtype)

    k = k_ref[0, 0]
    ve = ve_ref[...]
    tq = q_ref.shape[2]
    chunks = [(g, r * rc, rc) for g in range(Q_PER_KV) for r in range(tq // rc)]
    (g0, r0, n0), (g1, r1, n1) = chunks[0], chunks[-1]
    chunks = ([(g0, r0, n0 // 2), (g0, r0 + n0 // 2, n0 // 2)] + chunks[1:-1]
              + [(g1, r1, n1 // 2), (g1, r1 + n1 // 2, n1 // 2)])
    for g, r, n in chunks:
        rows = pl.ds(r, n)
        sc = lax.dot_general(q_ref[0, g, rows, :], k, (((1,), (1,)), ((), ())), preferred_element_type=F32)
        m = jnp.max(sc, axis=-1, keepdims=True)
        p = jnp.exp2(sc - m).astype(BF16)
        oe = _dot(p, ve)
        o = oe[:, :HEAD_DIM] / oe[:, HEAD_DIM:]
        o_ref[0, rows, g * HEAD_DIM:(g + 1) * HEAD_DIM] = o.astype(o_ref.dtype)


def _attn_call(q, k, v, *, tq=2048, rc=256):
    b, n_q, s, _ = q.shape
    n_kv = k.shape[1]
    gw = Q_PER_KV * HEAD_DIM
    return pl.pallas_call(
        functools.partial(_attn_kernel, rc=rc),
        grid=(b, n_kv, s // tq),
        in_specs=[pl.BlockSpec((1, Q_PER_KV, tq, HEAD_DIM), lambda bi, h, i: (bi, h, i, 0)),
                  pl.BlockSpec((1, 1, s, HEAD_DIM), lambda bi, h, i: (bi, h, 0, 0)),
                  pl.BlockSpec((1, 1, s, HEAD_DIM), lambda bi, h, i: (bi, h, 0, 0))],
        out_specs=pl.BlockSpec((1, tq, gw), lambda bi, h, i: (bi, i, h)),
        out_shape=jax.ShapeDtypeStruct((b, s, n_q * HEAD_DIM), BF16),
        scratch_shapes=[pltpu.VMEM((s, 2 * HEAD_DIM), BF16)],
        compiler_params=_cparams(3),
        name="gqa_attention",
    )(q, k, v)


def _merge_kernel(x_ref, a_ref, c_ref, wga_ref, wgc_ref, wa_ref, wc_ref, o_ref,
                  wgab_ref, wgcb_ref, wab_ref, wcb_ref):
    step = pl.program_id(1)
    _cast_first(step, wga_ref, wgab_ref)
    _cast_first(step, wgc_ref, wgcb_ref)
    _cast_first(step, wa_ref, wab_ref)
    _cast_first(step, wc_ref, wcb_ref)
    x = x_ref[...]
    ga = jax.nn.sigmoid(_dot(x, wgab_ref[...]))
    gc = jax.nn.sigmoid(_dot(x, wgcb_ref[...]))
    merged = ga * _dot(a_ref[...], wab_ref[...]) + gc * _dot(c_ref[...], wcb_ref[...])
    o_ref[...] = merged.astype(o_ref.dtype)


def _merge_call(xn, attn, conv, w_in, layer, w_attn_br, w_conv_br, *, ga_col0, tm=512, tn=512):
    m, d = xn.shape
    cdim = conv.shape[1]
    ga0 = ga_col0 // tn
    gc0 = ga0 + d // tn
    return pl.pallas_call(
        _merge_kernel,
        grid=(d // tn, m // tm),
        in_specs=[pl.BlockSpec((tm, d), lambda j, i: (i, 0)),
                  pl.BlockSpec((tm, d), lambda j, i: (i, 0)),
                  pl.BlockSpec((tm, cdim), lambda j, i: (i, 0)),
                  pl.BlockSpec((None, d, tn), lambda j, i: (layer, 0, ga0 + j)),
                  pl.BlockSpec((None, d, tn), lambda j, i: (layer, 0, gc0 + j)),
                  pl.BlockSpec((None, d, tn), lambda j, i: (layer, 0, j)),
                  pl.BlockSpec((None, cdim, tn), lambda j, i: (layer, 0, j))],
        out_specs=pl.BlockSpec((tm, tn), lambda j, i: (i, j)),
        out_shape=jax.ShapeDtypeStruct((m, d), BF16),
        scratch_shapes=[pltpu.VMEM((d, tn), BF16), pltpu.VMEM((d, tn), BF16), pltpu.VMEM((d, tn), BF16),
                        pltpu.VMEM((cdim, tn), BF16)],
        compiler_params=_cparams(2),
        name="gate_merge",
    )(xn, attn, conv, w_in, w_in, w_attn_br, w_conv_br)


def _resid_norm_kernel(a_ref, w_ref, h_ref, g_ref, ho_ref, no_ref, wb_ref, *, rc):
    _cast_first(pl.program_id(0), w_ref, wb_ref)
    g = g_ref[...]
    for c in range(a_ref.shape[0] // rc):
        rows = pl.ds(c * rc, rc)
        h = h_ref[rows, :] + _dot(a_ref[rows, :], wb_ref[...])
        ho_ref[rows, :] = h
        no_ref[rows, :] = _rms(h, g).astype(no_ref.dtype)


def _resid_norm_call(a, w, layer, h, g, tm=512, rc=256):
    m, d = h.shape
    k = a.shape[1]
    row = lambda i: (i, 0)
    return pl.pallas_call(
        functools.partial(_resid_norm_kernel, rc=rc),
        grid=(m // tm,),
        in_specs=[pl.BlockSpec((tm, k), row),
                  pl.BlockSpec((None, k, d), lambda i: (layer, 0, 0), pipeline_mode=SINGLE),
                  pl.BlockSpec((tm, d), row),
                  pl.BlockSpec((1, d), lambda i: (0, 0))],
        out_specs=[pl.BlockSpec((tm, d), row), pl.BlockSpec((tm, d), row)],
        out_shape=[jax.ShapeDtypeStruct((m, d), F32), jax.ShapeDtypeStruct((m, d), BF16)],
        scratch_shapes=[pltpu.VMEM((k, d), BF16)],
        compiler_params=_cparams(1),
        name="mix_out_resid",
    )(a, w, h, g)


def _ffn_in_kernel(x_ref, wg_ref, wu_ref, cw_ref, cb_ref, o_ref, wgb_ref, wub_ref, pad_ref, *, rc):
    step = pl.program_id(1)
    _cast_first(step, wg_ref, wgb_ref)
    _cast_first(step, wu_ref, wub_ref)
    s = x_ref.shape[1]
    _zero_halo(pad_ref, s)
    for c in range(s // rc):
        pad_ref[pl.ds(CONV_PAD + c * rc, rc), :] = _dot(x_ref[0, pl.ds(c * rc, rc), :], wgb_ref[...])
    w = cw_ref[...]
    bias = cb_ref[...]
    for c in range(s // rc):
        rows = pl.ds(c * rc, rc)
        gate = _dwconv3_rows(pad_ref, c, rc, w) + bias
        up = _dot(x_ref[0, rows, :], wub_ref[...])
        o_ref[0, rows, :] = (gate * jax.nn.sigmoid(gate) * up).astype(o_ref.dtype)


def _ffn_in_call(hn, w_ffn_in, layer, conv_w, conv_b, *, d_ff, tn=512, rc=512):
    b, s, d = hn.shape
    nt = d_ff // tn
    return pl.pallas_call(
        functools.partial(_ffn_in_kernel, rc=rc),
        grid=(nt, b),
        in_specs=[pl.BlockSpec((1, s, d), lambda j, bi: (bi, 0, 0)),
                  pl.BlockSpec((None, d, tn), lambda j, bi: (layer, 0, j)),
                  pl.BlockSpec((None, d, tn), lambda j, bi: (layer, 0, nt + j)),
                  pl.BlockSpec((3, tn), lambda j, bi: (0, j)),
                  pl.BlockSpec((1, tn), lambda j, bi: (0, j))],
        out_specs=pl.BlockSpec((1, s, tn), lambda j, bi: (bi, 0, j)),
        out_shape=jax.ShapeDtypeStruct((b, s, d_ff), BF16),
        scratch_shapes=[pltpu.VMEM((d, tn), BF16), pltpu.VMEM((d, tn), BF16),
                        pltpu.VMEM((s + 2 * CONV_PAD, tn), F32)],
        compiler_params=_cparams(2),
        name="ffn_in",
    )(hn, w_ffn_in, w_ffn_in, conv_w, conv_b)


def _ffn_out_kernel(a_ref, w_ref, h_ref, ho_ref, wb_ref):
    _cast_first(pl.program_id(1), w_ref, wb_ref)
    ho_ref[...] = h_ref[...] + _dot(a_ref[...], wb_ref[...])


def _ffn_out_call(act, w, layer, h, tm=512, tn=1024):
    m, d = h.shape
    k = act.shape[1]
    return pl.pallas_call(
        _ffn_out_kernel,
        grid=(d // tn, m // tm),
        in_specs=[pl.BlockSpec((tm, k), lambda j, i: (i, 0)),
                  pl.BlockSpec((None, k, tn), lambda j, i: (layer, 0, j), pipeline_mode=SINGLE),
                  pl.BlockSpec((tm, tn), lambda j, i: (i, j))],
        out_specs=pl.BlockSpec((tm, tn), lambda j, i: (i, j)),
        out_shape=jax.ShapeDtypeStruct((m, d), F32),
        scratch_shapes=[pltpu.VMEM((k, tn), BF16)],
        compiler_params=_cparams(2),
        name="ffn_out_resid",
    )(act, w, h)


def _ple_kernel(h_ref, gp_ref, wg_ref, p_ref, wp_ref, gn_ref, *refs, last, rc):
    out_refs, (wgb_ref, wpb_ref) = refs[:-2], refs[-2:]
    _cast_first(pl.program_id(0), wg_ref, wgb_ref)
    _cast_first(pl.program_id(0), wp_ref, wpb_ref)
    gp = gp_ref[...]
    gn = gn_ref[...]
    for c in range(h_ref.shape[0] // rc):
        rows = pl.ds(c * rc, rc)
        h = h_ref[rows, :]
        gate = jax.nn.sigmoid(_dot(_rms(h, gp).astype(BF16), wgb_ref[...]))
        emb = _dot(p_ref[rows, :].astype(BF16), wpb_ref[...])
        h = h + gate * emb
        if last:
            out_refs[0][rows, :] = _rms(h, gn)
        else:
            out_refs[0][rows, :] = h
            out_refs[1][rows, :] = _rms(h, gn).astype(out_refs[1].dtype)


def _ple_call(h, g_ple, w_gate, p, layer, w_ple, g_next, *, last, tm=512, rc=128):
    m, d = h.shape
    pd = p.shape[-1]
    row = lambda i: (i, 0)
    const = lambda i: (0, 0)
    if last:
        out_specs = [pl.BlockSpec((tm, d), row)]
        out_shape = [jax.ShapeDtypeStruct((m, d), F32)]
    else:
        out_specs = [pl.BlockSpec((tm, d), row), pl.BlockSpec((tm, d), row)]
        out_shape = [jax.ShapeDtypeStruct((m, d), F32), jax.ShapeDtypeStruct((m, d), BF16)]
    return pl.pallas_call(
        functools.partial(_ple_kernel, last=last, rc=rc),
        grid=(m // tm,),
        in_specs=[pl.BlockSpec((tm, d), row),
                  pl.BlockSpec((1, d), const),
                  pl.BlockSpec((None, d, d), lambda i: (layer, 0, 0), pipeline_mode=SINGLE),
                  pl.BlockSpec((None, tm, pd), lambda i: (layer, i, 0)),
                  pl.BlockSpec((None, pd, d), lambda i: (layer, 0, 0), pipeline_mode=SINGLE),
                  pl.BlockSpec((1, d), const)],
        out_specs=out_specs,
        out_shape=out_shape,
        scratch_shapes=[pltpu.VMEM((d, d), BF16), pltpu.VMEM((pd, d), BF16)],
        compiler_params=_cparams(1),
        name="ple_final" if last else "ple",
    )(h, g_ple, w_gate, p, w_ple, g_next)


def _rope_tables(seq):
    axis_dim = HEAD_DIM // 2
    rows = seq // GRID_W
    row_ids = jnp.repeat(jnp.arange(rows, dtype=F32), GRID_W)
    col_ids = jnp.tile(jnp.arange(GRID_W, dtype=F32), rows)
    inv_freq = 1.0 / (ROPE_THETA ** (jnp.arange(0, axis_dim, 2, dtype=F32) / axis_dim))
    ang_r = row_ids[:, None] * inv_freq[None, :]
    ang_c = col_ids[:, None] * inv_freq[None, :]
    ang = jnp.concatenate([ang_r, ang_c, ang_r, ang_c], axis=-1)
    sign = jnp.where(jnp.arange(HEAD_DIM) < axis_dim, -1.0, 1.0).astype(F32)
    return jnp.cos(ang), jnp.sin(ang) * sign[None, :]


def kernel(x, p, norm_mix_g, w_in, q_norm_g, k_norm_g, w_attn_br, conv_mix_w, conv_mix_b, w_conv_br, w_mix_out, norm_ffn_g, w_ffn_in, conv_ffn_w, conv_ffn_b, w_ffn_out, norm_ple_g, w_ple_gate, w_ple, final_norm_g):
    b, s, d = x.shape
    depth = w_in.shape[0]
    m = b * s
    n_kv = d // HEAD_DIM // Q_PER_KV
    kv_dim = n_kv * HEAD_DIM
    conv_dim = conv_mix_w.shape[-1]
    d_ff = conv_ffn_w.shape[-1]
    conv_col0 = d + 2 * kv_dim
    gate_col0 = conv_col0 + 3 * conv_dim
    pd = p.shape[-1]

    cos, sin = _rope_tables(s)
    p2 = p.reshape(depth, m, pd)

    h = x.reshape(m, d)
    xn = _norm_call(h, norm_mix_g[0].reshape(1, d))
    out = None
    for i in range(depth):
        xn3 = xn.reshape(b, s, d)
        q = _q_call(xn3, w_in, i, cos, sin, q_norm_g[i].reshape(1, HEAD_DIM))
        k, v = _kv_call(xn3, w_in, i, cos, sin, k_norm_g[i].reshape(1, HEAD_DIM), k_col0=d, n_kv=n_kv)
        conv = _convbr_call(xn3, w_in, i, conv_mix_w[i], conv_mix_b[i].reshape(1, conv_dim),
                            col0=conv_col0, width=conv_dim)
        attn = _attn_call(q, k, v)
        merged = _merge_call(xn, attn.reshape(m, d), conv.reshape(m, conv_dim), w_in, i,
                             w_attn_br, w_conv_br, ga_col0=gate_col0)
        h, hn = _resid_norm_call(merged, w_mix_out, i, h, norm_ffn_g[i].reshape(1, d))
        act = _ffn_in_call(hn.reshape(b, s, d), w_ffn_in, i, conv_ffn_w[i],
                           conv_ffn_b[i].reshape(1, d_ff), d_ff=d_ff)
        h = _ffn_out_call(act.reshape(m, d_ff), w_ffn_out, i, h)
        last = i == depth - 1
        g_next = final_norm_g if last else norm_mix_g[i + 1]
        res = _ple_call(h, norm_ple_g[i].reshape(1, d), w_ple_gate, p2, i, w_ple, g_next.reshape(1, d),
                        last=last)
        if last:
            out = res[0]
        else:
            h, xn = res
    return out.reshape(b, s, d)
```

```python
import functools

import numpy as np
import jax
import jax.numpy as jnp
from jax import lax
from jax.experimental import pallas as pl
from jax.experimental.pallas import tpu as pltpu

HEAD_DIM = 128
Q_PER_KV = 4
ROPE_THETA = 10000.0
GRID_W = 64
EPS = 1e-6
LOG2E = 1.4426950408889634
CONV_PAD = 8
V7X_VMEM_LIMIT = 60 * 1024 * 1024

F32 = jnp.float32
BF16 = jnp.bfloat16
SINGLE = pl.Buffered(1)


def _cparams(n_axes):
    return pltpu.CompilerParams(dimension_semantics=("arbitrary",) * n_axes,
                                vmem_limit_bytes=V7X_VMEM_LIMIT)


def _rms(x, g):
    ms = jnp.mean(x * x, axis=-1, keepdims=True)
    return x * lax.rsqrt(ms + EPS) * g


def _dot(a, b):
    return jnp.dot(a, b, preferred_element_type=F32)


def _cast_first(step, w_ref, wb_ref):
    @pl.when(step == 0)
    def _():
        wb_ref[...] = w_ref[...].astype(wb_ref.dtype)


def _norm_kernel(x_ref, g_ref, o_ref):
    o_ref[...] = _rms(x_ref[...], g_ref[...]).astype(o_ref.dtype)


def _norm_call(x, g, tm=512):
    m, d = x.shape
    return pl.pallas_call(
        _norm_kernel,
        grid=(m // tm,),
        in_specs=[pl.BlockSpec((tm, d), lambda i: (i, 0)),
                  pl.BlockSpec((1, d), lambda i: (0, 0))],
        out_specs=pl.BlockSpec((tm, d), lambda i: (i, 0)),
        out_shape=jax.ShapeDtypeStruct((m, d), BF16),
        compiler_params=_cparams(1),
        name="init_norm",
    )(x, g)


def _swap_mid(x):
    lane = lax.broadcasted_iota(jnp.int32, x.shape, x.ndim - 1)
    seg = lane // (HEAD_DIM // 4)
    up = pltpu.roll(x, 3 * HEAD_DIM // 4, x.ndim - 1)
    dn = pltpu.roll(x, HEAD_DIM // 4, x.ndim - 1)
    return jnp.where(seg == 1, up, jnp.where(seg == 2, dn, x))


def _cast_swapped_first(step, w_ref, wb_ref):
    @pl.when(step == 0)
    def _():
        for hh in range(w_ref.shape[1] // HEAD_DIM):
            cols = slice(hh * HEAD_DIM, (hh + 1) * HEAD_DIM)
            wb_ref[:, cols] = _swap_mid(w_ref[:, cols]).astype(wb_ref.dtype)


def _rope_gain_tables(g_ref, scale, cos, sgn_sin):
    g = _swap_mid(jnp.broadcast_to(g_ref[...], (8, HEAD_DIM)))[0:1] * scale
    return cos * g, sgn_sin * pltpu.roll(jnp.broadcast_to(g, (8, HEAD_DIM)), HEAD_DIM // 2, 1)[0:1]


def _head_mean_matrix():
    r = lax.broadcasted_iota(jnp.int32, (2 * HEAD_DIM, 2 * HEAD_DIM), 0) // HEAD_DIM
    c = lax.broadcasted_iota(jnp.int32, (2 * HEAD_DIM, 2 * HEAD_DIM), 1) // HEAD_DIM
    return jnp.where(r == c, 1.0 / HEAD_DIM, 0.0).astype(BF16)


def _norm_rope_heads(z, ca, sb, bd, o_ref, rows):
    z2 = (z * z).astype(BF16)
    for pair in range(z.shape[1] // (2 * HEAD_DIM)):
        rs = lax.rsqrt(_dot(z2[:, pair * 2 * HEAD_DIM:(pair + 1) * 2 * HEAD_DIM], bd) + EPS)
        for sub in range(2):
            hh = 2 * pair + sub
            x = z[:, hh * HEAD_DIM:(hh + 1) * HEAD_DIM]
            y = (x * ca + pltpu.roll(x, HEAD_DIM // 2, 1) * sb) * rs[:, sub * HEAD_DIM:(sub + 1) * HEAD_DIM]
            o_ref[0, hh, rows, :] = y.astype(o_ref.dtype)


def _q_kernel(x_ref, w_ref, cos_ref, sin_ref, g_ref, o_ref, wb_ref, *, rc):
    _cast_swapped_first(pl.program_id(1), w_ref, wb_ref)
    s = x_ref.shape[1]
    scale = LOG2E / float(np.sqrt(HEAD_DIM))
    bd = _head_mean_matrix()
    for c in range(s // rc):
        rows = pl.ds(c * rc, rc)
        ca, sb = _rope_gain_tables(g_ref, scale, cos_ref[rows, :], sin_ref[rows, :])
        _norm_rope_heads(_dot(x_ref[0, rows, :], wb_ref[...]), ca, sb, bd, o_ref, rows)


def _q_call(xn, w_in, layer, cos, sin, qg, *, tn=1024, rc=256):
    b, s, d = xn.shape
    hpt = tn // HEAD_DIM
    const = lambda j, bi: (0, 0)
    return pl.pallas_call(
        functools.partial(_q_kernel, rc=rc),
        grid=(d // tn, b),
        in_specs=[pl.BlockSpec((1, s, d), lambda j, bi: (bi, 0, 0)),
                  pl.BlockSpec((None, d, tn), lambda j, bi: (layer, 0, j)),
                  pl.BlockSpec((s, HEAD_DIM), const),
                  pl.BlockSpec((s, HEAD_DIM), const),
                  pl.BlockSpec((1, HEAD_DIM), const)],
        out_specs=pl.BlockSpec((1, hpt, s, HEAD_DIM), lambda j, bi: (bi, j, 0, 0)),
        out_shape=jax.ShapeDtypeStruct((b, d // HEAD_DIM, s, HEAD_DIM), BF16),
        scratch_shapes=[pltpu.VMEM((d, tn), BF16)],
        compiler_params=_cparams(2),
        name="q_proj",
    )(xn, w_in, cos, sin, qg)


def _kv_kernel(x_ref, wk_ref, wv_ref, cos_ref, sin_ref, g_ref, k_ref, v_ref, wkb_ref, wvb_ref, *, rc):
    _cast_swapped_first(pl.program_id(0), wk_ref, wkb_ref)
    _cast_first(pl.program_id(0), wv_ref, wvb_ref)
    s = x_ref.shape[1]
    n_kv = k_ref.shape[1]
    bd = _head_mean_matrix()
    for c in range(s // rc):
        rows = pl.ds(c * rc, rc)
        x = x_ref[0, rows, :]
        ca, sb = _rope_gain_tables(g_ref, 1.0, cos_ref[rows, :], sin_ref[rows, :])
        _norm_rope_heads(_dot(x, wkb_ref[...]), ca, sb, bd, k_ref, rows)
        zv = _dot(x, wvb_ref[...])
        for hh in range(n_kv):
            v_ref[0, hh, rows, :] = zv[:, hh * HEAD_DIM:(hh + 1) * HEAD_DIM].astype(v_ref.dtype)


def _kv_call(xn, w_in, layer, cos, sin, kg, *, k_col0, n_kv, rc=512):
    b, s, d = xn.shape
    kv = n_kv * HEAD_DIM
    kt = k_col0 // kv
    const = lambda bi: (0, 0)
    return pl.pallas_call(
        functools.partial(_kv_kernel, rc=rc),
        grid=(b,),
        in_specs=[pl.BlockSpec((1, s, d), lambda bi: (bi, 0, 0)),
                  pl.BlockSpec((None, d, kv), lambda bi: (layer, 0, kt), pipeline_mode=SINGLE),
                  pl.BlockSpec((None, d, kv), lambda bi: (layer, 0, kt + 1), pipeline_mode=SINGLE),
                  pl.BlockSpec((s, HEAD_DIM), const),
                  pl.BlockSpec((s, HEAD_DIM), const),
                  pl.BlockSpec((1, HEAD_DIM), const)],
        out_specs=[pl.BlockSpec((1, n_kv, s, HEAD_DIM), lambda bi: (bi, 0, 0, 0)),
                   pl.BlockSpec((1, n_kv, s, HEAD_DIM), lambda bi: (bi, 0, 0, 0))],
        out_shape=[jax.ShapeDtypeStruct((b, n_kv, s, HEAD_DIM), BF16),
                   jax.ShapeDtypeStruct((b, n_kv, s, HEAD_DIM), BF16)],
        scratch_shapes=[pltpu.VMEM((d, kv), BF16), pltpu.VMEM((d, kv), BF16)],
        compiler_params=_cparams(1),
        name="kv_proj",
    )(xn, w_in, w_in, cos, sin, kg)


def _dwconv3_rows(pad_ref, c, rc, w):
    base = CONV_PAD + c * rc
    prev = pad_ref[pl.ds(base - 1, rc), :]
    cur = pad_ref[pl.ds(base, rc), :]
    nxt = pad_ref[pl.ds(base + 1, rc), :]
    return prev * w[0:1, :] + cur * w[1:2, :] + nxt * w[2:3, :]


def _zero_halo(pad_ref, s):
    zeros = jnp.zeros((CONV_PAD, pad_ref.shape[1]), pad_ref.dtype)
    pad_ref[pl.ds(0, CONV_PAD), :] = zeros
    pad_ref[pl.ds(CONV_PAD + s, CONV_PAD), :] = zeros


def _convbr_kernel(x_ref, wb_ref, wc_ref, wu_ref, cw_ref, cb_ref, o_ref, wbb_ref, wcb_ref, wub_ref, pad_ref,
                   *, rc):
    step = pl.program_id(1)
    _cast_first(step, wb_ref, wbb_ref)
    _cast_first(step, wc_ref, wcb_ref)
    _cast_first(step, wu_ref, wub_ref)
    s = x_ref.shape[1]
    _zero_halo(pad_ref, s)
    for c in range(s // rc):
        x = x_ref[0, pl.ds(c * rc, rc), :]
        pad_ref[pl.ds(CONV_PAD + c * rc, rc), :] = _dot(x, wcb_ref[...]) * _dot(x, wub_ref[...])
    w = cw_ref[...]
    bias = cb_ref[...]
    for c in range(s // rc):
        rows = pl.ds(c * rc, rc)
        conv = _dwconv3_rows(pad_ref, c, rc, w) + bias
        bg = _dot(x_ref[0, rows, :], wbb_ref[...])
        o_ref[0, rows, :] = (bg * conv).astype(o_ref.dtype)


def _convbr_call(xn, w_in, layer, conv_w, conv_b, *, col0, width, tn=256, rc=512):
    b, s, d = xn.shape
    t0 = col0 // tn
    nt = width // tn
    wspec = lambda off: pl.BlockSpec((None, d, tn), lambda j, bi: (layer, 0, t0 + off * nt + j))
    return pl.pallas_call(
        functools.partial(_convbr_kernel, rc=rc),
        grid=(nt, b),
        in_specs=[pl.BlockSpec((1, s, d), lambda j, bi: (bi, 0, 0)),
                  wspec(0), wspec(1), wspec(2),
                  pl.BlockSpec((3, tn), lambda j, bi: (0, j)),
                  pl.BlockSpec((1, tn), lambda j, bi: (0, j))],
        out_specs=pl.BlockSpec((1, s, tn), lambda j, bi: (bi, 0, j)),
        out_shape=jax.ShapeDtypeStruct((b, s, width), BF16),
        scratch_shapes=[pltpu.VMEM((d, tn), BF16), pltpu.VMEM((d, tn), BF16), pltpu.VMEM((d, tn), BF16),
                        pltpu.VMEM((s + 2 * CONV_PAD, tn), F32)],
        compiler_params=_cparams(2),
        name="conv_branch",
    )(xn, w_in, w_in, w_in, conv_w, conv_b)


def _attn_kernel(q_ref, k_ref, v_ref, o_ref, ve_ref, *, rc):
    s = k_ref.shape[2]
    tq = q_ref.shape[2]

    def build_ve():
        ve_ref[:, :HEAD_DIM] = v_ref[0, 0]
        ve_ref[:, HEAD_DIM:] = jnp.ones((s, HEAD_DIM), ve_ref.dtype)

    if tq == s:
        build_ve()
    else:
        pl.when(pl.program_id(2) == 0)(build_ve)

    k = k_ref[0, 0]
    ve = ve_ref[...]
    chunks = [(g, r * rc, rc) for g in range(Q_PER_KV) for r in range(tq // rc)]
    (g0, r0, n0), (g1, r1, n1) = chunks[0], chunks[-1]
    chunks = ([(g0, r0, n0 // 2), (g0, r0 + n0 // 2, n0 // 2)] + chunks[1:-1]
              + [(g1, r1, n1 // 2), (g1, r1 + n1 // 2, n1 // 2)])
    for g, r, n in chunks:
        rows = pl.ds(r, n)
        sc = lax.dot_general(q_ref[0, g, rows, :], k, (((1,), (1,)), ((), ())), preferred_element_type=F32)
        m = jnp.max(sc, axis=-1, keepdims=True)
        p = jnp.exp2(sc - m).astype(BF16)
        oe = _dot(p, ve)
        o = oe[:, :HEAD_DIM] / oe[:, HEAD_DIM:]
        o_ref[0, rows, g * HEAD_DIM:(g + 1) * HEAD_DIM] = o.astype(o_ref.dtype)


def _attn_call(q, k, v, *, tq=2048, rc=256):
    b, n_q, s, _ = q.shape
    n_kv = k.shape[1]
    gw = Q_PER_KV * HEAD_DIM
    return pl.pallas_call(
        functools.partial(_attn_kernel, rc=rc),
        grid=(b, n_kv, s // tq),
        in_specs=[pl.BlockSpec((1, Q_PER_KV, tq, HEAD_DIM), lambda bi, h, i: (bi, h, i, 0)),
                  pl.BlockSpec((1, 1, s, HEAD_DIM), lambda bi, h, i: (bi, h, 0, 0)),
                  pl.BlockSpec((1, 1, s, HEAD_DIM), lambda bi, h, i: (bi, h, 0, 0))],
        out_specs=pl.BlockSpec((1, tq, gw), lambda bi, h, i: (bi, i, h)),
        out_shape=jax.ShapeDtypeStruct((b, s, n_q * HEAD_DIM), BF16),
        scratch_shapes=[pltpu.VMEM((s, 2 * HEAD_DIM), BF16)],
        compiler_params=_cparams(3),
        name="gqa_attention",
    )(q, k, v)


def _merge_kernel(x_ref, a_ref, c_ref, wga_ref, wgc_ref, wa_ref, wc_ref, o_ref,
                  wgab_ref, wgcb_ref, wab_ref, wcb_ref):
    step = pl.program_id(1)
    _cast_first(step, wga_ref, wgab_ref)
    _cast_first(step, wgc_ref, wgcb_ref)
    _cast_first(step, wa_ref, wab_ref)
    _cast_first(step, wc_ref, wcb_ref)
    x = x_ref[...]
    ga = jax.nn.sigmoid(_dot(x, wgab_ref[...]))
    gc = jax.nn.sigmoid(_dot(x, wgcb_ref[...]))
    merged = ga * _dot(a_ref[...], wab_ref[...]) + gc * _dot(c_ref[...], wcb_ref[...])
    o_ref[...] = merged.astype(o_ref.dtype)


def _merge_call(xn, attn, conv, w_in, layer, w_attn_br, w_conv_br, *, ga_col0, tm=512, tn=512):
    m, d = xn.shape
    cdim = conv.shape[1]
    ga0 = ga_col0 // tn
    gc0 = ga0 + d // tn
    return pl.pallas_call(
        _merge_kernel,
        grid=(d // tn, m // tm),
        in_specs=[pl.BlockSpec((tm, d), lambda j, i: (i, 0)),
                  pl.BlockSpec((tm, d), lambda j, i: (i, 0)),
                  pl.BlockSpec((tm, cdim), lambda j, i: (i, 0)),
                  pl.BlockSpec((None, d, tn), lambda j, i: (layer, 0, ga0 + j)),
                  pl.BlockSpec((None, d, tn), lambda j, i: (layer, 0, gc0 + j)),
                  pl.BlockSpec((None, d, tn), lambda j, i: (layer, 0, j)),
                  pl.BlockSpec((None, cdim, tn), lambda j, i: (layer, 0, j))],
        out_specs=pl.BlockSpec((tm, tn), lambda j, i: (i, j)),
        out_shape=jax.ShapeDtypeStruct((m, d), BF16),
        scratch_shapes=[pltpu.VMEM((d, tn), BF16), pltpu.VMEM((d, tn), BF16), pltpu.VMEM((d, tn), BF16),
                        pltpu.VMEM((cdim, tn), BF16)],
        compiler_params=_cparams(2),
        name="gate_merge",
    )(xn, attn, conv, w_in, w_in, w_attn_br, w_conv_br)


def _resid_norm_kernel(a_ref, w_ref, h_ref, g_ref, ho_ref, no_ref, wb_ref, *, rc):
    _cast_first(pl.program_id(0), w_ref, wb_ref)
    g = g_ref[...]
    for c in range(a_ref.shape[0] // rc):
        rows = pl.ds(c * rc, rc)
        h = h_ref[rows, :] + _dot(a_ref[rows, :], wb_ref[...])
        ho_ref[rows, :] = h
        no_ref[rows, :] = _rms(h, g).astype(no_ref.dtype)


def _resid_norm_call(a, w, layer, h, g, tm=512, rc=256):
    m, d = h.shape
    k = a.shape[1]
    row = lambda i: (i, 0)
    return pl.pallas_call(
        functools.partial(_resid_norm_kernel, rc=rc),
        grid=(m // tm,),
        in_specs=[pl.BlockSpec((tm, k), row),
                  pl.BlockSpec((None, k, d), lambda i: (layer, 0, 0), pipeline_mode=SINGLE),
                  pl.BlockSpec((tm, d), row),
                  pl.BlockSpec((1, d), lambda i: (0, 0))],
        out_specs=[pl.BlockSpec((tm, d), row), pl.BlockSpec((tm, d), row)],
        out_shape=[jax.ShapeDtypeStruct((m, d), F32), jax.ShapeDtypeStruct((m, d), BF16)],
        scratch_shapes=[pltpu.VMEM((k, d), BF16)],
        compiler_params=_cparams(1),
        name="mix_out_resid",
    )(a, w, h, g)


def _ffn_in_kernel(x_ref, wg_ref, wu_ref, cw_ref, cb_ref, o_ref, wgb_ref, wub_ref, pad_ref, *, rc):
    step = pl.program_id(1)
    _cast_first(step, wg_ref, wgb_ref)
    _cast_first(step, wu_ref, wub_ref)
    s = x_ref.shape[1]
    _zero_halo(pad_ref, s)
    for c in range(s // rc):
        pad_ref[pl.ds(CONV_PAD + c * rc, rc), :] = _dot(x_ref[0, pl.ds(c * rc, rc), :], wgb_ref[...])
    w = cw_ref[...]
    bias = cb_ref[...]
    for c in range(s // rc):
        rows = pl.ds(c * rc, rc)
        gate = _dwconv3_rows(pad_ref, c, rc, w) + bias
        up = _dot(x_ref[0, rows, :], wub_ref[...])
        o_ref[0, rows, :] = (gate * jax.nn.sigmoid(gate) * up).astype(o_ref.dtype)


def _ffn_in_call(hn, w_ffn_in, layer, conv_w, conv_b, *, d_ff, tn=512, rc=512):
    b, s, d = hn.shape
    nt = d_ff // tn
    return pl.pallas_call(
        functools.partial(_ffn_in_kernel, rc=rc),
        grid=(nt, b),
        in_specs=[pl.BlockSpec((1, s, d), lambda j, bi: (bi, 0, 0)),
                  pl.BlockSpec((None, d, tn), lambda j, bi: (layer, 0, j)),
                  pl.BlockSpec((None, d, tn), lambda j, bi: (layer, 0, nt + j)),
                  pl.BlockSpec((3, tn), lambda j, bi: (0, j)),
                  pl.BlockSpec((1, tn), lambda j, bi: (0, j))],
        out_specs=pl.BlockSpec((1, s, tn), lambda j, bi: (bi, 0, j)),
        out_shape=jax.ShapeDtypeStruct((b, s, d_ff), BF16),
        scratch_shapes=[pltpu.VMEM((d, tn), BF16), pltpu.VMEM((d, tn), BF16),
                        pltpu.VMEM((s + 2 * CONV_PAD, tn), F32)],
        compiler_params=_cparams(2),
        name="ffn_in",
    )(hn, w_ffn_in, w_ffn_in, conv_w, conv_b)


def _ffn_out_kernel(a_ref, w_ref, h_ref, ho_ref, wb_ref):
    _cast_first(pl.program_id(1), w_ref, wb_ref)
    ho_ref[...] = h_ref[...] + _dot(a_ref[...], wb_ref[...])


def _ffn_out_call(act, w, layer, h, tm=512, tn=1024):
    m, d = h.shape
    k = act.shape[1]
    return pl.pallas_call(
        _ffn_out_kernel,
        grid=(d // tn, m // tm),
        in_specs=[pl.BlockSpec((tm, k), lambda j, i: (i, 0)),
                  pl.BlockSpec((None, k, tn), lambda j, i: (layer, 0, j), pipeline_mode=SINGLE),
                  pl.BlockSpec((tm, tn), lambda j, i: (i, j))],
        out_specs=pl.BlockSpec((tm, tn), lambda j, i: (i, j)),
        out_shape=jax.ShapeDtypeStruct((m, d), F32),
        scratch_shapes=[pltpu.VMEM((k, tn), BF16)],
        compiler_params=_cparams(2),
        name="ffn_out_resid",
    )(act, w, h)


def _ple_kernel(h_ref, gp_ref, wg_ref, p_ref, wp_ref, gn_ref, *refs, last, rc):
    out_refs, (wgb_ref, wpb_ref) = refs[:-2], refs[-2:]
    _cast_first(pl.program_id(0), wg_ref, wgb_ref)
    _cast_first(pl.program_id(0), wp_ref, wpb_ref)
    gp = gp_ref[...]
    gn = gn_ref[...]
    for c in range(h_ref.shape[0] // rc):
        rows = pl.ds(c * rc, rc)
        h = h_ref[rows, :]
        gate = jax.nn.sigmoid(_dot(_rms(h, gp).astype(BF16), wgb_ref[...]))
        emb = _dot(p_ref[rows, :].astype(BF16), wpb_ref[...])
        h = h + gate * emb
        if last:
            out_refs[0][rows, :] = _rms(h, gn)
        else:
            out_refs[0][rows, :] = h
            out_refs[1][rows, :] = _rms(h, gn).astype(out_refs[1].dtype)


def _ple_call(h, g_ple, w_gate, p, layer, w_ple, g_next, *, last, tm=512, rc=128):
    m, d = h.shape
    pd = p.shape[-1]
    row = lambda i: (i, 0)
    const = lambda i: (0, 0)
    if last:
        out_specs = [pl.BlockSpec((tm, d), row)]
        out_shape = [jax.ShapeDtypeStruct((m, d), F32)]
    else:
        out_specs = [pl.BlockSpec((tm, d), row), pl.BlockSpec((tm, d), row)]
        out_shape = [jax.ShapeDtypeStruct((m, d), F32), jax.ShapeDtypeStruct((m, d), BF16)]
    return pl.pallas_call(
        functools.partial(_ple_kernel, last=last, rc=rc),
        grid=(m // tm,),
        in_specs=[pl.BlockSpec((tm, d), row),
                  pl.BlockSpec((1, d), const),
                  pl.BlockSpec((None, d, d), lambda i: (layer, 0, 0), pipeline_mode=SINGLE),
                  pl.BlockSpec((None, tm, pd), lambda i: (layer, i, 0)),
                  pl.BlockSpec((None, pd, d), lambda i: (layer, 0, 0), pipeline_mode=SINGLE),
                  pl.BlockSpec((1, d), const)],
        out_specs=out_specs,
        out_shape=out_shape,
        scratch_shapes=[pltpu.VMEM((d, d), BF16), pltpu.VMEM((pd, d), BF16)],
        compiler_params=_cparams(1),
        name="ple_final" if last else "ple",
    )(h, g_ple, w_gate, p, w_ple, g_next)


def _rope_tables(seq):
    axis_dim = HEAD_DIM // 2
    rows = seq // GRID_W
    row_ids = jnp.repeat(jnp.arange(rows, dtype=F32), GRID_W)
    col_ids = jnp.tile(jnp.arange(GRID_W, dtype=F32), rows)
    inv_freq = 1.0 / (ROPE_THETA ** (jnp.arange(0, axis_dim, 2, dtype=F32) / axis_dim))
    ang_r = row_ids[:, None] * inv_freq[None, :]
    ang_c = col_ids[:, None] * inv_freq[None, :]
    ang = jnp.concatenate([ang_r, ang_c, ang_r, ang_c], axis=-1)
    sign = jnp.where(jnp.arange(HEAD_DIM) < axis_dim, -1.0, 1.0).astype(F32)
    return jnp.cos(ang), jnp.sin(ang) * sign[None, :]


def kernel(x, p, norm_mix_g, w_in, q_norm_g, k_norm_g, w_attn_br, conv_mix_w, conv_mix_b, w_conv_br, w_mix_out, norm_ffn_g, w_ffn_in, conv_ffn_w, conv_ffn_b, w_ffn_out, norm_ple_g, w_ple_gate, w_ple, final_norm_g):
    b, s, d = x.shape
    depth = w_in.shape[0]
    m = b * s
    n_kv = d // HEAD_DIM // Q_PER_KV
    kv_dim = n_kv * HEAD_DIM
    conv_dim = conv_mix_w.shape[-1]
    d_ff = conv_ffn_w.shape[-1]
    conv_col0 = d + 2 * kv_dim
    gate_col0 = conv_col0 + 3 * conv_dim
    pd = p.shape[-1]

    cos, sin = _rope_tables(s)
    p2 = p.reshape(depth, m, pd)

    h = x.reshape(m, d)
    xn = _norm_call(h, norm_mix_g[0].reshape(1, d))
    out = None
    for i in range(depth):
        xn3 = xn.reshape(b, s, d)
        q = _q_call(xn3, w_in, i, cos, sin, q_norm_g[i].reshape(1, HEAD_DIM))
        k, v = _kv_call(xn3, w_in, i, cos, sin, k_norm_g[i].reshape(1, HEAD_DIM), k_col0=d, n_kv=n_kv)
        conv = _convbr_call(xn3, w_in, i, conv_mix_w[i], conv_mix_b[i].reshape(1, conv_dim),
                            col0=conv_col0, width=conv_dim)
        attn = _attn_call(q, k, v)
        merged = _merge_call(xn, attn.reshape(m, d), conv.reshape(m, conv_dim), w_in, i,
                             w_attn_br, w_conv_br, ga_col0=gate_col0)
        h, hn = _resid_norm_call(merged, w_mix_out, i, h, norm_ffn_g[i].reshape(1, d))
        act = _ffn_in_call(hn.reshape(b, s, d), w_ffn_in, i, conv_ffn_w[i],
                           conv_ffn_b[i].reshape(1, d_ff), d_ff=d_ff)
        h = _ffn_out_call(act.reshape(m, d_ff), w_ffn_out, i, h)
        last = i == depth - 1
        g_next = final_norm_g if last else norm_mix_g[i + 1]
        res = _ple_call(h, norm_ple_g[i].reshape(1, d), w_ple_gate, p2, i, w_ple, g_next.reshape(1, d),
                        last=last)
        if last:
            out = res[0]
        else:
            h, xn = res
    return out.reshape(b, s, d)
```

```python
import functools

import numpy as np
import jax
import jax.numpy as jnp
from jax import lax
from jax.experimental import pallas as pl
from jax.experimental.pallas import tpu as pltpu

HEAD_DIM = 128
Q_PER_KV = 4
ROPE_THETA = 10000.0
GRID_W = 64
EPS = 1e-6
LOG2E = 1.4426950408889634
CONV_PAD = 8
V7X_VMEM_LIMIT = 60 * 1024 * 1024

F32 = jnp.float32
BF16 = jnp.bfloat16
SINGLE = pl.Buffered(1)


def _cparams(n_axes):
    return pltpu.CompilerParams(dimension_semantics=("arbitrary",) * n_axes,
                                vmem_limit_bytes=V7X_VMEM_LIMIT)


def _rms(x, g):
    ms = jnp.mean(x * x, axis=-1, keepdims=True)
    return x * lax.rsqrt(ms + EPS) * g


def _dot(a, b):
    return jnp.dot(a, b, preferred_element_type=F32)


def _cast_first(step, w_ref, wb_ref):
    @pl.when(step == 0)
    def _():
        wb_ref[...] = w_ref[...].astype(wb_ref.dtype)


def _norm_kernel(x_ref, g_ref, o_ref):
    o_ref[...] = _rms(x_ref[...], g_ref[...]).astype(o_ref.dtype)


def _norm_call(x, g, tm=1024):
    m, d = x.shape
    return pl.pallas_call(
        _norm_kernel,
        grid=(m // tm,),
        in_specs=[pl.BlockSpec((tm, d), lambda i: (i, 0)),
                  pl.BlockSpec((1, d), lambda i: (0, 0))],
        out_specs=pl.BlockSpec((tm, d), lambda i: (i, 0)),
        out_shape=jax.ShapeDtypeStruct((m, d), BF16),
        compiler_params=_cparams(1),
        name="init_norm",
    )(x, g)


def _swap_mid(x):
    lane = lax.broadcasted_iota(jnp.int32, x.shape, x.ndim - 1)
    seg = lane // (HEAD_DIM // 4)
    up = pltpu.roll(x, 3 * HEAD_DIM // 4, x.ndim - 1)
    dn = pltpu.roll(x, HEAD_DIM // 4, x.ndim - 1)
    return jnp.where(seg == 1, up, jnp.where(seg == 2, dn, x))


def _cast_swapped_first(step, w_ref, wb_ref):
    @pl.when(step == 0)
    def _():
        for hh in range(w_ref.shape[1] // HEAD_DIM):
            cols = slice(hh * HEAD_DIM, (hh + 1) * HEAD_DIM)
            wb_ref[:, cols] = _swap_mid(w_ref[:, cols]).astype(wb_ref.dtype)


def _rope_gain_tables(g_ref, scale, cos, sgn_sin):
    g = _swap_mid(jnp.broadcast_to(g_ref[...], (8, HEAD_DIM)))[0:1] * scale
    return cos * g, sgn_sin * pltpu.roll(jnp.broadcast_to(g, (8, HEAD_DIM)), HEAD_DIM // 2, 1)[0:1]


def _head_mean_matrix():
    r = lax.broadcasted_iota(jnp.int32, (2 * HEAD_DIM, 2 * HEAD_DIM), 0) // HEAD_DIM
    c = lax.broadcasted_iota(jnp.int32, (2 * HEAD_DIM, 2 * HEAD_DIM), 1) // HEAD_DIM
    return jnp.where(r == c, 1.0 / HEAD_DIM, 0.0).astype(BF16)


def _norm_rope_heads(z, ca, sb, bd, o_ref, rows):
    z2 = (z * z).astype(BF16)
    for pair in range(z.shape[1] // (2 * HEAD_DIM)):
        rs = lax.rsqrt(_dot(z2[:, pair * 2 * HEAD_DIM:(pair + 1) * 2 * HEAD_DIM], bd) + EPS)
        for sub in range(2):
            hh = 2 * pair + sub
            x = z[:, hh * HEAD_DIM:(hh + 1) * HEAD_DIM]
            y = (x * ca + pltpu.roll(x, HEAD_DIM // 2, 1) * sb) * rs[:, sub * HEAD_DIM:(sub + 1) * HEAD_DIM]
            o_ref[0, hh, rows, :] = y.astype(o_ref.dtype)


def _q_kernel(x_ref, w_ref, cos_ref, sin_ref, g_ref, o_ref, wb_ref, *, rc):
    _cast_swapped_first(pl.program_id(1), w_ref, wb_ref)
    s = x_ref.shape[1]
    scale = LOG2E / float(np.sqrt(HEAD_DIM))
    bd = _head_mean_matrix()
    for c in range(s // rc):
        rows = pl.ds(c * rc, rc)
        ca, sb = _rope_gain_tables(g_ref, scale, cos_ref[rows, :], sin_ref[rows, :])
        _norm_rope_heads(_dot(x_ref[0, rows, :], wb_ref[...]), ca, sb, bd, o_ref, rows)


def _q_call(xn, w_in, layer, cos, sin, qg, *, tn=1024, rc=256):
    b, s, d = xn.shape
    hpt = tn // HEAD_DIM
    const = lambda j, bi: (0, 0)
    return pl.pallas_call(
        functools.partial(_q_kernel, rc=rc),
        grid=(d // tn, b),
        in_specs=[pl.BlockSpec((1, s, d), lambda j, bi: (bi, 0, 0)),
                  pl.BlockSpec((None, d, tn), lambda j, bi: (layer, 0, j)),
                  pl.BlockSpec((s, HEAD_DIM), const),
                  pl.BlockSpec((s, HEAD_DIM), const),
                  pl.BlockSpec((1, HEAD_DIM), const)],
        out_specs=pl.BlockSpec((1, hpt, s, HEAD_DIM), lambda j, bi: (bi, j, 0, 0)),
        out_shape=jax.ShapeDtypeStruct((b, d // HEAD_DIM, s, HEAD_DIM), BF16),
        scratch_shapes=[pltpu.VMEM((d, tn), BF16)],
        compiler_params=_cparams(2),
        name="q_proj",
    )(xn, w_in, cos, sin, qg)


def _kv_kernel(x_ref, wk_ref, wv_ref, cos_ref, sin_ref, g_ref, k_ref, v_ref, wkb_ref, wvb_ref, *, rc):
    _cast_swapped_first(pl.program_id(0), wk_ref, wkb_ref)
    _cast_first(pl.program_id(0), wv_ref, wvb_ref)
    s = x_ref.shape[1]
    n_kv = k_ref.shape[1]
    bd = _head_mean_matrix()
    for c in range(s // rc):
        rows = pl.ds(c * rc, rc)
        x = x_ref[0, rows, :]
        ca, sb = _rope_gain_tables(g_ref, 1.0, cos_ref[rows, :], sin_ref[rows, :])
        _norm_rope_heads(_dot(x, wkb_ref[...]), ca, sb, bd, k_ref, rows)
        zv = _dot(x, wvb_ref[...])
        for hh in range(n_kv):
            v_ref[0, hh, rows, :] = zv[:, hh * HEAD_DIM:(hh + 1) * HEAD_DIM].astype(v_ref.dtype)


def _kv_call(xn, w_in, layer, cos, sin, kg, *, k_col0, n_kv, rc=512):
    b, s, d = xn.shape
    kv = n_kv * HEAD_DIM
    kt = k_col0 // kv
    const = lambda bi: (0, 0)
    return pl.pallas_call(
        functools.partial(_kv_kernel, rc=rc),
        grid=(b,),
        in_specs=[pl.BlockSpec((1, s, d), lambda bi: (bi, 0, 0)),
                  pl.BlockSpec((None, d, kv), lambda bi: (layer, 0, kt), pipeline_mode=SINGLE),
                  pl.BlockSpec((None, d, kv), lambda bi: (layer, 0, kt + 1), pipeline_mode=SINGLE),
                  pl.BlockSpec((s, HEAD_DIM), const),
                  pl.BlockSpec((s, HEAD_DIM), const),
                  pl.BlockSpec((1, HEAD_DIM), const)],
        out_specs=[pl.BlockSpec((1, n_kv, s, HEAD_DIM), lambda bi: (bi, 0, 0, 0)),
                   pl.BlockSpec((1, n_kv, s, HEAD_DIM), lambda bi: (bi, 0, 0, 0))],
        out_shape=[jax.ShapeDtypeStruct((b, n_kv, s, HEAD_DIM), BF16),
                   jax.ShapeDtypeStruct((b, n_kv, s, HEAD_DIM), BF16)],
        scratch_shapes=[pltpu.VMEM((d, kv), BF16), pltpu.VMEM((d, kv), BF16)],
        compiler_params=_cparams(1),
        name="kv_proj",
    )(xn, w_in, w_in, cos, sin, kg)


def _dwconv3_rows(pad_ref, c, rc, w):
    base = CONV_PAD + c * rc
    prev = pad_ref[pl.ds(base - 1, rc), :]
    cur = pad_ref[pl.ds(base, rc), :]
    nxt = pad_ref[pl.ds(base + 1, rc), :]
    return prev * w[0:1, :] + cur * w[1:2, :] + nxt * w[2:3, :]


def _zero_halo(pad_ref, s):
    zeros = jnp.zeros((CONV_PAD, pad_ref.shape[1]), pad_ref.dtype)
    pad_ref[pl.ds(0, CONV_PAD), :] = zeros
    pad_ref[pl.ds(CONV_PAD + s, CONV_PAD), :] = zeros


def _convbr_kernel(x_ref, wb_ref, wc_ref, wu_ref, cw_ref, cb_ref, o_ref, wbb_ref, wcb_ref, wub_ref, pad_ref,
                   *, rc):
    step = pl.program_id(1)
    _cast_first(step, wb_ref, wbb_ref)
    _cast_first(step, wc_ref, wcb_ref)
    _cast_first(step, wu_ref, wub_ref)
    s = x_ref.shape[1]
    _zero_halo(pad_ref, s)
    for c in range(s // rc):
        x = x_ref[0, pl.ds(c * rc, rc), :]
        pad_ref[pl.ds(CONV_PAD + c * rc, rc), :] = _dot(x, wcb_ref[...]) * _dot(x, wub_ref[...])
    w = cw_ref[...]
    bias = cb_ref[...]
    for c in range(s // rc):
        rows = pl.ds(c * rc, rc)
        conv = _dwconv3_rows(pad_ref, c, rc, w) + bias
        bg = _dot(x_ref[0, rows, :], wbb_ref[...])
        o_ref[0, rows, :] = (bg * conv).astype(o_ref.dtype)


def _convbr_call(xn, w_in, layer, conv_w, conv_b, *, col0, width, tn=256, rc=512):
    b, s, d = xn.shape
    t0 = col0 // tn
    nt = width // tn
    wspec = lambda off: pl.BlockSpec((None, d, tn), lambda j, bi: (layer, 0, t0 + off * nt + j))
    return pl.pallas_call(
        functools.partial(_convbr_kernel, rc=rc),
        grid=(nt, b),
        in_specs=[pl.BlockSpec((1, s, d), lambda j, bi: (bi, 0, 0)),
                  wspec(0), wspec(1), wspec(2),
                  pl.BlockSpec((3, tn), lambda j, bi: (0, j)),
                  pl.BlockSpec((1, tn), lambda j, bi: (0, j))],
        out_specs=pl.BlockSpec((1, s, tn), lambda j, bi: (bi, 0, j)),
        out_shape=jax.ShapeDtypeStruct((b, s, width), BF16),
        scratch_shapes=[pltpu.VMEM((d, tn), BF16), pltpu.VMEM((d, tn), BF16), pltpu.VMEM((d, tn), BF16),
                        pltpu.VMEM((s + 2 * CONV_PAD, tn), F32)],
        compiler_params=_cparams(2),
        name="conv_branch",
    )(xn, w_in, w_in, w_in, conv_w, conv_b)


def _attn_kernel(q_ref, k_ref, v_ref, o_ref, ve_ref, *, rc):
    s = k_ref.shape[2]
    tq = q_ref.shape[2]

    def build_ve():
        ve_ref[:, :HEAD_DIM] = v_ref[0, 0]
        ve_ref[:, HEAD_DIM:] = jnp.ones((s, HEAD_DIM), ve_ref.dtype)

    if tq == s:
        build_ve()
    else:
        pl.when(pl.program_id(2) == 0)(build_ve)

    k = k_ref[0, 0]
    ve = ve_ref[...]
    for g in range(Q_PER_KV):
        for r in range(tq // rc):
            rows = pl.ds(r * rc, rc)
            sc = lax.dot_general(q_ref[0, g, rows, :], k, (((1,), (1,)), ((), ())),
                                 preferred_element_type=F32)
            m = jnp.max(sc, axis=-1, keepdims=True)
            p = jnp.exp2(sc - m).astype(BF16)
            oe = _dot(p, ve)
            o = oe[:, :HEAD_DIM] / oe[:, HEAD_DIM:]
            o_ref[0, rows, g * HEAD_DIM:(g + 1) * HEAD_DIM] = o.astype(o_ref.dtype)


def _attn_call(q, k, v, *, tq=2048, rc=256):
    b, n_q, s, _ = q.shape
    n_kv = k.shape[1]
    gw = Q_PER_KV * HEAD_DIM
    return pl.pallas_call(
        functools.partial(_attn_kernel, rc=rc),
        grid=(b, n_kv, s // tq),
        in_specs=[pl.BlockSpec((1, Q_PER_KV, tq, HEAD_DIM), lambda bi, h, i: (bi, h, i, 0)),
                  pl.BlockSpec((1, 1, s, HEAD_DIM), lambda bi, h, i: (bi, h, 0, 0)),
                  pl.BlockSpec((1, 1, s, HEAD_DIM), lambda bi, h, i: (bi, h, 0, 0))],
        out_specs=pl.BlockSpec((1, tq, gw), lambda bi, h, i: (bi, i, h)),
        out_shape=jax.ShapeDtypeStruct((b, s, n_q * HEAD_DIM), BF16),
        scratch_shapes=[pltpu.VMEM((s, 2 * HEAD_DIM), BF16)],
        compiler_params=_cparams(3),
        name="gqa_attention",
    )(q, k, v)


def _merge_kernel(x_ref, a_ref, c_ref, wga_ref, wgc_ref, wa_ref, wc_ref, o_ref,
                  wgab_ref, wgcb_ref, wab_ref, wcb_ref):
    step = pl.program_id(1)
    _cast_first(step, wga_ref, wgab_ref)
    _cast_first(step, wgc_ref, wgcb_ref)
    _cast_first(step, wa_ref, wab_ref)
    _cast_first(step, wc_ref, wcb_ref)
    x = x_ref[...]
    ga = jax.nn.sigmoid(_dot(x, wgab_ref[...]))
    gc = jax.nn.sigmoid(_dot(x, wgcb_ref[...]))
    merged = ga * _dot(a_ref[...], wab_ref[...]) + gc * _dot(c_ref[...], wcb_ref[...])
    o_ref[...] = merged.astype(o_ref.dtype)


def _merge_call(xn, attn, conv, w_in, layer, w_attn_br, w_conv_br, *, ga_col0, tm=512, tn=512):
    m, d = xn.shape
    cdim = conv.shape[1]
    ga0 = ga_col0 // tn
    gc0 = ga0 + d // tn
    return pl.pallas_call(
        _merge_kernel,
        grid=(d // tn, m // tm),
        in_specs=[pl.BlockSpec((tm, d), lambda j, i: (i, 0)),
                  pl.BlockSpec((tm, d), lambda j, i: (i, 0)),
                  pl.BlockSpec((tm, cdim), lambda j, i: (i, 0)),
                  pl.BlockSpec((None, d, tn), lambda j, i: (layer, 0, ga0 + j)),
                  pl.BlockSpec((None, d, tn), lambda j, i: (layer, 0, gc0 + j)),
                  pl.BlockSpec((None, d, tn), lambda j, i: (layer, 0, j)),
                  pl.BlockSpec((None, cdim, tn), lambda j, i: (layer, 0, j))],
        out_specs=pl.BlockSpec((tm, tn), lambda j, i: (i, j)),
        out_shape=jax.ShapeDtypeStruct((m, d), BF16),
        scratch_shapes=[pltpu.VMEM((d, tn), BF16), pltpu.VMEM((d, tn), BF16), pltpu.VMEM((d, tn), BF16),
                        pltpu.VMEM((cdim, tn), BF16)],
        compiler_params=_cparams(2),
        name="gate_merge",
    )(xn, attn, conv, w_in, w_in, w_attn_br, w_conv_br)


def _resid_norm_kernel(a_ref, w_ref, h_ref, g_ref, ho_ref, no_ref, wb_ref, *, rc):
    _cast_first(pl.program_id(0), w_ref, wb_ref)
    g = g_ref[...]
    for c in range(a_ref.shape[0] // rc):
        rows = pl.ds(c * rc, rc)
        h = h_ref[rows, :] + _dot(a_ref[rows, :], wb_ref[...])
        ho_ref[rows, :] = h
        no_ref[rows, :] = _rms(h, g).astype(no_ref.dtype)


def _resid_norm_call(a, w, layer, h, g, tm=512, rc=256):
    m, d = h.shape
    k = a.shape[1]
    row = lambda i: (i, 0)
    return pl.pallas_call(
        functools.partial(_resid_norm_kernel, rc=rc),
        grid=(m // tm,),
        in_specs=[pl.BlockSpec((tm, k), row),
                  pl.BlockSpec((None, k, d), lambda i: (layer, 0, 0), pipeline_mode=SINGLE),
                  pl.BlockSpec((tm, d), row),
                  pl.BlockSpec((1, d), lambda i: (0, 0))],
        out_specs=[pl.BlockSpec((tm, d), row), pl.BlockSpec((tm, d), row)],
        out_shape=[jax.ShapeDtypeStruct((m, d), F32), jax.ShapeDtypeStruct((m, d), BF16)],
        scratch_shapes=[pltpu.VMEM((k, d), BF16)],
        compiler_params=_cparams(1),
        name="mix_out_resid",
    )(a, w, h, g)


def _ffn_in_kernel(x_ref, wg_ref, wu_ref, cw_ref, cb_ref, o_ref, wgb_ref, wub_ref, pad_ref, *, rc):
    step = pl.program_id(1)
    _cast_first(step, wg_ref, wgb_ref)
    _cast_first(step, wu_ref, wub_ref)
    s = x_ref.shape[1]
    _zero_halo(pad_ref, s)
    for c in range(s // rc):
        pad_ref[pl.ds(CONV_PAD + c * rc, rc), :] = _dot(x_ref[0, pl.ds(c * rc, rc), :], wgb_ref[...])
    w = cw_ref[...]
    bias = cb_ref[...]
    for c in range(s // rc):
        rows = pl.ds(c * rc, rc)
        gate = _dwconv3_rows(pad_ref, c, rc, w) + bias
        up = _dot(x_ref[0, rows, :], wub_ref[...])
        o_ref[0, rows, :] = (gate * jax.nn.sigmoid(gate) * up).astype(o_ref.dtype)


def _ffn_in_call(hn, w_ffn_in, layer, conv_w, conv_b, *, d_ff, tn=512, rc=512):
    b, s, d = hn.shape
    nt = d_ff // tn
    return pl.pallas_call(
        functools.partial(_ffn_in_kernel, rc=rc),
        grid=(nt, b),
        in_specs=[pl.BlockSpec((1, s, d), lambda j, bi: (bi, 0, 0)),
                  pl.BlockSpec((None, d, tn), lambda j, bi: (layer, 0, j)),
                  pl.BlockSpec((None, d, tn), lambda j, bi: (layer, 0, nt + j)),
                  pl.BlockSpec((3, tn), lambda j, bi: (0, j)),
                  pl.BlockSpec((1, tn), lambda j, bi: (0, j))],
        out_specs=pl.BlockSpec((1, s, tn), lambda j, bi: (bi, 0, j)),
        out_shape=jax.ShapeDtypeStruct((b, s, d_ff), BF16),
        scratch_shapes=[pltpu.VMEM((d, tn), BF16), pltpu.VMEM((d, tn), BF16),
                        pltpu.VMEM((s + 2 * CONV_PAD, tn), F32)],
        compiler_params=_cparams(2),
        name="ffn_in",
    )(hn, w_ffn_in, w_ffn_in, conv_w, conv_b)


def _ffn_out_kernel(a_ref, w_ref, h_ref, ho_ref, wb_ref):
    _cast_first(pl.program_id(1), w_ref, wb_ref)
    ho_ref[...] = h_ref[...] + _dot(a_ref[...], wb_ref[...])


def _ffn_out_call(act, w, layer, h, tm=512, tn=1024):
    m, d = h.shape
    k = act.shape[1]
    return pl.pallas_call(
        _ffn_out_kernel,
        grid=(d // tn, m // tm),
        in_specs=[pl.BlockSpec((tm, k), lambda j, i: (i, 0)),
                  pl.BlockSpec((None, k, tn), lambda j, i: (layer, 0, j), pipeline_mode=SINGLE),
                  pl.BlockSpec((tm, tn), lambda j, i: (i, j))],
        out_specs=pl.BlockSpec((tm, tn), lambda j, i: (i, j)),
        out_shape=jax.ShapeDtypeStruct((m, d), F32),
        scratch_shapes=[pltpu.VMEM((k, tn), BF16)],
        compiler_params=_cparams(2),
        name="ffn_out_resid",
    )(act, w, h)


def _ple_kernel(h_ref, gp_ref, wg_ref, p_ref, wp_ref, gn_ref, *refs, last, rc):
    out_refs, (wgb_ref, wpb_ref) = refs[:-2], refs[-2:]
    _cast_first(pl.program_id(0), wg_ref, wgb_ref)
    _cast_first(pl.program_id(0), wp_ref, wpb_ref)
    gp = gp_ref[...]
    gn = gn_ref[...]
    for c in range(h_ref.shape[0] // rc):
        rows = pl.ds(c * rc, rc)
        h = h_ref[rows, :]
        gate = jax.nn.sigmoid(_dot(_rms(h, gp).astype(BF16), wgb_ref[...]))
        emb = _dot(p_ref[rows, :].astype(BF16), wpb_ref[...])
        h = h + gate * emb
        if last:
            out_refs[0][rows, :] = _rms(h, gn)
        else:
            out_refs[0][rows, :] = h
            out_refs[1][rows, :] = _rms(h, gn).astype(out_refs[1].dtype)


def _ple_call(h, g_ple, w_gate, p, layer, w_ple, g_next, *, last, tm=512, rc=128):
    m, d = h.shape
    pd = p.shape[-1]
    row = lambda i: (i, 0)
    const = lambda i: (0, 0)
    if last:
        out_specs = [pl.BlockSpec((tm, d), row)]
        out_shape = [jax.ShapeDtypeStruct((m, d), F32)]
    else:
        out_specs = [pl.BlockSpec((tm, d), row), pl.BlockSpec((tm, d), row)]
        out_shape = [jax.ShapeDtypeStruct((m, d), F32), jax.ShapeDtypeStruct((m, d), BF16)]
    return pl.pallas_call(
        functools.partial(_ple_kernel, last=last, rc=rc),
        grid=(m // tm,),
        in_specs=[pl.BlockSpec((tm, d), row),
                  pl.BlockSpec((1, d), const),
                  pl.BlockSpec((None, d, d), lambda i: (layer, 0, 0), pipeline_mode=SINGLE),
                  pl.BlockSpec((None, tm, pd), lambda i: (layer, i, 0)),
                  pl.BlockSpec((None, pd, d), lambda i: (layer, 0, 0), pipeline_mode=SINGLE),
                  pl.BlockSpec((1, d), const)],
        out_specs=out_specs,
        out_shape=out_shape,
        scratch_shapes=[pltpu.VMEM((d, d), BF16), pltpu.VMEM((pd, d), BF16)],
        compiler_params=_cparams(1),
        name="ple_final" if last else "ple",
    )(h, g_ple, w_gate, p, w_ple, g_next)


def _rope_tables(seq):
    axis_dim = HEAD_DIM // 2
    rows = seq // GRID_W
    row_ids = jnp.repeat(jnp.arange(rows, dtype=F32), GRID_W)
    col_ids = jnp.tile(jnp.arange(GRID_W, dtype=F32), rows)
    inv_freq = 1.0 / (ROPE_THETA ** (jnp.arange(0, axis_dim, 2, dtype=F32) / axis_dim))
    ang_r = row_ids[:, None] * inv_freq[None, :]
    ang_c = col_ids[:, None] * inv_freq[None, :]
    ang = jnp.concatenate([ang_r, ang_c, ang_r, ang_c], axis=-1)
    sign = jnp.where(jnp.arange(HEAD_DIM) < axis_dim, -1.0, 1.0).astype(F32)
    return jnp.cos(ang), jnp.sin(ang) * sign[None, :]


def kernel(x, p, norm_mix_g, w_in, q_norm_g, k_norm_g, w_attn_br, conv_mix_w, conv_mix_b, w_conv_br, w_mix_out, norm_ffn_g, w_ffn_in, conv_ffn_w, conv_ffn_b, w_ffn_out, norm_ple_g, w_ple_gate, w_ple, final_norm_g):
    b, s, d = x.shape
    depth = w_in.shape[0]
    m = b * s
    n_kv = d // HEAD_DIM // Q_PER_KV
    kv_dim = n_kv * HEAD_DIM
    conv_dim = conv_mix_w.shape[-1]
    d_ff = conv_ffn_w.shape[-1]
    conv_col0 = d + 2 * kv_dim
    gate_col0 = conv_col0 + 3 * conv_dim
    pd = p.shape[-1]

    cos, sin = _rope_tables(s)
    p2 = p.reshape(depth, m, pd)

    h = x.reshape(m, d)
    xn = _norm_call(h, norm_mix_g[0].reshape(1, d))
    out = None
    for i in range(depth):
        xn3 = xn.reshape(b, s, d)
        q = _q_call(xn3, w_in, i, cos, sin, q_norm_g[i].reshape(1, HEAD_DIM))
        k, v = _kv_call(xn3, w_in, i, cos, sin, k_norm_g[i].reshape(1, HEAD_DIM), k_col0=d, n_kv=n_kv)
        conv = _convbr_call(xn3, w_in, i, conv_mix_w[i], conv_mix_b[i].reshape(1, conv_dim),
                            col0=conv_col0, width=conv_dim)
        attn = _attn_call(q, k, v)
        merged = _merge_call(xn, attn.reshape(m, d), conv.reshape(m, conv_dim), w_in, i,
                             w_attn_br, w_conv_br, ga_col0=gate_col0)
        h, hn = _resid_norm_call(merged, w_mix_out, i, h, norm_ffn_g[i].reshape(1, d))
        act = _ffn_in_call(hn.reshape(b, s, d), w_ffn_in, i, conv_ffn_w[i],
                           conv_ffn_b[i].reshape(1, d_ff), d_ff=d_ff)
        h = _ffn_out_call(act.reshape(m, d_ff), w_ffn_out, i, h)
        last = i == depth - 1
        g_next = final_norm_g if last else norm_mix_g[i + 1]
        res = _ple_call(h, norm_ple_g[i].reshape(1, d), w_ple_gate, p2, i, w_ple, g_next.reshape(1, d),
                        last=last)
        if last:
            out = res[0]
        else:
            h, xn = res
    return out.reshape(b, s, d)
```

```python
import functools

import numpy as np
import jax
import jax.numpy as jnp
from jax import lax
from jax.experimental import pallas as pl
from jax.experimental.pallas import tpu as pltpu

HEAD_DIM = 128
Q_PER_KV = 4
ROPE_THETA = 10000.0
GRID_W = 64
EPS = 1e-6
LOG2E = 1.4426950408889634
CONV_PAD = 8
V7X_VMEM_LIMIT = 60 * 1024 * 1024

F32 = jnp.float32
BF16 = jnp.bfloat16
SINGLE = pl.Buffered(1)


def _cparams(n_axes):
    return pltpu.CompilerParams(dimension_semantics=("arbitrary",) * n_axes,
                                vmem_limit_bytes=V7X_VMEM_LIMIT)


def _rms(x, g):
    ms = jnp.mean(x * x, axis=-1, keepdims=True)
    return x * lax.rsqrt(ms + EPS) * g


def _dot(a, b):
    return jnp.dot(a, b, preferred_element_type=F32)


def _cast_first(step, w_ref, wb_ref):
    @pl.when(step == 0)
    def _():
        wb_ref[...] = w_ref[...].astype(wb_ref.dtype)


def _norm_kernel(x_ref, g_ref, o_ref):
    o_ref[...] = _rms(x_ref[...], g_ref[...]).astype(o_ref.dtype)


def _norm_call(x, g, tm=1024):
    m, d = x.shape
    return pl.pallas_call(
        _norm_kernel,
        grid=(m // tm,),
        in_specs=[pl.BlockSpec((tm, d), lambda i: (i, 0)),
                  pl.BlockSpec((1, d), lambda i: (0, 0))],
        out_specs=pl.BlockSpec((tm, d), lambda i: (i, 0)),
        out_shape=jax.ShapeDtypeStruct((m, d), BF16),
        compiler_params=_cparams(1),
        name="init_norm",
    )(x, g)


def _swap_mid(x):
    lane = lax.broadcasted_iota(jnp.int32, x.shape, x.ndim - 1)
    seg = lane // (HEAD_DIM // 4)
    up = pltpu.roll(x, 3 * HEAD_DIM // 4, x.ndim - 1)
    dn = pltpu.roll(x, HEAD_DIM // 4, x.ndim - 1)
    return jnp.where(seg == 1, up, jnp.where(seg == 2, dn, x))


def _cast_swapped_first(step, w_ref, wb_ref):
    @pl.when(step == 0)
    def _():
        for hh in range(w_ref.shape[1] // HEAD_DIM):
            cols = slice(hh * HEAD_DIM, (hh + 1) * HEAD_DIM)
            wb_ref[:, cols] = _swap_mid(w_ref[:, cols]).astype(wb_ref.dtype)


def _rope_gain_tables(g_ref, scale, cos, sgn_sin):
    g = _swap_mid(jnp.broadcast_to(g_ref[...], (8, HEAD_DIM)))[0:1] * scale
    return cos * g, sgn_sin * pltpu.roll(jnp.broadcast_to(g, (8, HEAD_DIM)), HEAD_DIM // 2, 1)[0:1]


def _head_mean_matrix():
    r = lax.broadcasted_iota(jnp.int32, (2 * HEAD_DIM, 2 * HEAD_DIM), 0) // HEAD_DIM
    c = lax.broadcasted_iota(jnp.int32, (2 * HEAD_DIM, 2 * HEAD_DIM), 1) // HEAD_DIM
    return jnp.where(r == c, 1.0 / HEAD_DIM, 0.0).astype(BF16)


def _norm_rope_heads(z, ca, sb, bd, o_ref, rows):
    z2 = (z * z).astype(BF16)
    for pair in range(z.shape[1] // (2 * HEAD_DIM)):
        rs = lax.rsqrt(_dot(z2[:, pair * 2 * HEAD_DIM:(pair + 1) * 2 * HEAD_DIM], bd) + EPS)
        for sub in range(2):
            hh = 2 * pair + sub
            x = z[:, hh * HEAD_DIM:(hh + 1) * HEAD_DIM]
            y = (x * ca + pltpu.roll(x, HEAD_DIM // 2, 1) * sb) * rs[:, sub * HEAD_DIM:(sub + 1) * HEAD_DIM]
            o_ref[0, hh, rows, :] = y.astype(o_ref.dtype)


def _q_kernel(x_ref, w_ref, cos_ref, sin_ref, g_ref, o_ref, wb_ref, *, rc):
    _cast_swapped_first(pl.program_id(1), w_ref, wb_ref)
    s = x_ref.shape[1]
    scale = LOG2E / float(np.sqrt(HEAD_DIM))
    bd = _head_mean_matrix()
    for c in range(s // rc):
        rows = pl.ds(c * rc, rc)
        ca, sb = _rope_gain_tables(g_ref, scale, cos_ref[rows, :], sin_ref[rows, :])
        _norm_rope_heads(_dot(x_ref[0, rows, :], wb_ref[...]), ca, sb, bd, o_ref, rows)


def _q_call(xn, w_in, layer, cos, sin, qg, *, tn=1024, rc=256):
    b, s, d = xn.shape
    hpt = tn // HEAD_DIM
    const = lambda j, bi: (0, 0)
    return pl.pallas_call(
        functools.partial(_q_kernel, rc=rc),
        grid=(d // tn, b),
        in_specs=[pl.BlockSpec((1, s, d), lambda j, bi: (bi, 0, 0)),
                  pl.BlockSpec((None, d, tn), lambda j, bi: (layer, 0, j)),
                  pl.BlockSpec((s, HEAD_DIM), const),
                  pl.BlockSpec((s, HEAD_DIM), const),
                  pl.BlockSpec((1, HEAD_DIM), const)],
        out_specs=pl.BlockSpec((1, hpt, s, HEAD_DIM), lambda j, bi: (bi, j, 0, 0)),
        out_shape=jax.ShapeDtypeStruct((b, d // HEAD_DIM, s, HEAD_DIM), BF16),
        scratch_shapes=[pltpu.VMEM((d, tn), BF16)],
        compiler_params=_cparams(2),
        name="q_proj",
    )(xn, w_in, cos, sin, qg)


def _kv_kernel(x_ref, wk_ref, wv_ref, cos_ref, sin_ref, g_ref, k_ref, v_ref, wkb_ref, wvb_ref, *, rc):
    _cast_swapped_first(pl.program_id(0), wk_ref, wkb_ref)
    _cast_first(pl.program_id(0), wv_ref, wvb_ref)
    s = x_ref.shape[1]
    n_kv = k_ref.shape[1]
    bd = _head_mean_matrix()
    for c in range(s // rc):
        rows = pl.ds(c * rc, rc)
        x = x_ref[0, rows, :]
        ca, sb = _rope_gain_tables(g_ref, 1.0, cos_ref[rows, :], sin_ref[rows, :])
        _norm_rope_heads(_dot(x, wkb_ref[...]), ca, sb, bd, k_ref, rows)
        zv = _dot(x, wvb_ref[...])
        for hh in range(n_kv):
            v_ref[0, hh, rows, :] = zv[:, hh * HEAD_DIM:(hh + 1) * HEAD_DIM].astype(v_ref.dtype)


def _kv_call(xn, w_in, layer, cos, sin, kg, *, k_col0, n_kv, rc=512):
    b, s, d = xn.shape
    kv = n_kv * HEAD_DIM
    kt = k_col0 // kv
    const = lambda bi: (0, 0)
    return pl.pallas_call(
        functools.partial(_kv_kernel, rc=rc),
        grid=(b,),
        in_specs=[pl.BlockSpec((1, s, d), lambda bi: (bi, 0, 0)),
                  pl.BlockSpec((None, d, kv), lambda bi: (layer, 0, kt), pipeline_mode=SINGLE),
                  pl.BlockSpec((None, d, kv), lambda bi: (layer, 0, kt + 1), pipeline_mode=SINGLE),
                  pl.BlockSpec((s, HEAD_DIM), const),
                  pl.BlockSpec((s, HEAD_DIM), const),
                  pl.BlockSpec((1, HEAD_DIM), const)],
        out_specs=[pl.BlockSpec((1, n_kv, s, HEAD_DIM), lambda bi: (bi, 0, 0, 0)),
                   pl.BlockSpec((1, n_kv, s, HEAD_DIM), lambda bi: (bi, 0, 0, 0))],
        out_shape=[jax.ShapeDtypeStruct((b, n_kv, s, HEAD_DIM), BF16),
                   jax.ShapeDtypeStruct((b, n_kv, s, HEAD_DIM), BF16)],
        scratch_shapes=[pltpu.VMEM((d, kv), BF16), pltpu.VMEM((d, kv), BF16)],
        compiler_params=_cparams(1),
        name="kv_proj",
    )(xn, w_in, w_in, cos, sin, kg)


def _dwconv3_rows(pad_ref, c, rc, w):
    base = CONV_PAD + c * rc
    prev = pad_ref[pl.ds(base - 1, rc), :]
    cur = pad_ref[pl.ds(base, rc), :]
    nxt = pad_ref[pl.ds(base + 1, rc), :]
    return prev * w[0:1, :] + cur * w[1:2, :] + nxt * w[2:3, :]


def _zero_halo(pad_ref, s):
    zeros = jnp.zeros((CONV_PAD, pad_ref.shape[1]), pad_ref.dtype)
    pad_ref[pl.ds(0, CONV_PAD), :] = zeros
    pad_ref[pl.ds(CONV_PAD + s, CONV_PAD), :] = zeros


def _convbr_kernel(x_ref, wb_ref, wc_ref, wu_ref, cw_ref, cb_ref, o_ref, wbb_ref, wcb_ref, wub_ref, pad_ref,
                   *, rc):
    step = pl.program_id(1)
    _cast_first(step, wb_ref, wbb_ref)
    _cast_first(step, wc_ref, wcb_ref)
    _cast_first(step, wu_ref, wub_ref)
    s = x_ref.shape[1]
    _zero_halo(pad_ref, s)
    for c in range(s // rc):
        x = x_ref[0, pl.ds(c * rc, rc), :]
        pad_ref[pl.ds(CONV_PAD + c * rc, rc), :] = _dot(x, wcb_ref[...]) * _dot(x, wub_ref[...])
    w = cw_ref[...]
    bias = cb_ref[...]
    for c in range(s // rc):
        rows = pl.ds(c * rc, rc)
        conv = _dwconv3_rows(pad_ref, c, rc, w) + bias
        bg = _dot(x_ref[0, rows, :], wbb_ref[...])
        o_ref[0, rows, :] = (bg * conv).astype(o_ref.dtype)


def _convbr_call(xn, w_in, layer, conv_w, conv_b, *, col0, width, tn=512, rc=512):
    b, s, d = xn.shape
    t0 = col0 // tn
    nt = width // tn
    wspec = lambda off: pl.BlockSpec((None, d, tn), lambda j, bi: (layer, 0, t0 + off * nt + j),
                                     pipeline_mode=SINGLE)
    return pl.pallas_call(
        functools.partial(_convbr_kernel, rc=rc),
        grid=(nt, b),
        in_specs=[pl.BlockSpec((1, s, d), lambda j, bi: (bi, 0, 0)),
                  wspec(0), wspec(1), wspec(2),
                  pl.BlockSpec((3, tn), lambda j, bi: (0, j)),
                  pl.BlockSpec((1, tn), lambda j, bi: (0, j))],
        out_specs=pl.BlockSpec((1, s, tn), lambda j, bi: (bi, 0, j)),
        out_shape=jax.ShapeDtypeStruct((b, s, width), BF16),
        scratch_shapes=[pltpu.VMEM((d, tn), BF16), pltpu.VMEM((d, tn), BF16), pltpu.VMEM((d, tn), BF16),
                        pltpu.VMEM((s + 2 * CONV_PAD, tn), F32)],
        compiler_params=_cparams(2),
        name="conv_branch",
    )(xn, w_in, w_in, w_in, conv_w, conv_b)


def _attn_kernel(q_ref, k_ref, v_ref, o_ref, ve_ref, *, rc):
    s = k_ref.shape[2]
    tq = q_ref.shape[2]

    def build_ve():
        ve_ref[:, :HEAD_DIM] = v_ref[0, 0]
        ve_ref[:, HEAD_DIM:] = jnp.ones((s, HEAD_DIM), ve_ref.dtype)

    if tq == s:
        build_ve()
    else:
        pl.when(pl.program_id(2) == 0)(build_ve)

    k = k_ref[0, 0]
    ve = ve_ref[...]
    for g in range(Q_PER_KV):
        for r in range(tq // rc):
            rows = pl.ds(r * rc, rc)
            sc = lax.dot_general(q_ref[0, g, rows, :], k, (((1,), (1,)), ((), ())),
                                 preferred_element_type=F32)
            m = jnp.max(sc, axis=-1, keepdims=True)
            p = jnp.exp2(sc - m).astype(BF16)
            oe = _dot(p, ve)
            o = oe[:, :HEAD_DIM] / oe[:, HEAD_DIM:]
            o_ref[0, rows, g * HEAD_DIM:(g + 1) * HEAD_DIM] = o.astype(o_ref.dtype)


def _attn_call(q, k, v, *, tq=2048, rc=256):
    b, n_q, s, _ = q.shape
    n_kv = k.shape[1]
    gw = Q_PER_KV * HEAD_DIM
    return pl.pallas_call(
        functools.partial(_attn_kernel, rc=rc),
        grid=(b, n_kv, s // tq),
        in_specs=[pl.BlockSpec((1, Q_PER_KV, tq, HEAD_DIM), lambda bi, h, i: (bi, h, i, 0)),
                  pl.BlockSpec((1, 1, s, HEAD_DIM), lambda bi, h, i: (bi, h, 0, 0)),
                  pl.BlockSpec((1, 1, s, HEAD_DIM), lambda bi, h, i: (bi, h, 0, 0))],
        out_specs=pl.BlockSpec((1, tq, gw), lambda bi, h, i: (bi, i, h)),
        out_shape=jax.ShapeDtypeStruct((b, s, n_q * HEAD_DIM), BF16),
        scratch_shapes=[pltpu.VMEM((s, 2 * HEAD_DIM), BF16)],
        compiler_params=_cparams(3),
        name="gqa_attention",
    )(q, k, v)


def _merge_kernel(x_ref, a_ref, c_ref, wga_ref, wgc_ref, wa_ref, wc_ref, o_ref,
                  wgab_ref, wgcb_ref, wab_ref, wcb_ref):
    step = pl.program_id(1)
    _cast_first(step, wga_ref, wgab_ref)
    _cast_first(step, wgc_ref, wgcb_ref)
    _cast_first(step, wa_ref, wab_ref)
    _cast_first(step, wc_ref, wcb_ref)
    x = x_ref[...]
    ga = jax.nn.sigmoid(_dot(x, wgab_ref[...]))
    gc = jax.nn.sigmoid(_dot(x, wgcb_ref[...]))
    merged = ga * _dot(a_ref[...], wab_ref[...]) + gc * _dot(c_ref[...], wcb_ref[...])
    o_ref[...] = merged.astype(o_ref.dtype)


def _merge_call(xn, attn, conv, w_in, layer, w_attn_br, w_conv_br, *, ga_col0, tm=512, tn=512):
    m, d = xn.shape
    cdim = conv.shape[1]
    ga0 = ga_col0 // tn
    gc0 = ga0 + d // tn
    return pl.pallas_call(
        _merge_kernel,
        grid=(d // tn, m // tm),
        in_specs=[pl.BlockSpec((tm, d), lambda j, i: (i, 0)),
                  pl.BlockSpec((tm, d), lambda j, i: (i, 0)),
                  pl.BlockSpec((tm, cdim), lambda j, i: (i, 0)),
                  pl.BlockSpec((None, d, tn), lambda j, i: (layer, 0, ga0 + j)),
                  pl.BlockSpec((None, d, tn), lambda j, i: (layer, 0, gc0 + j)),
                  pl.BlockSpec((None, d, tn), lambda j, i: (layer, 0, j)),
                  pl.BlockSpec((None, cdim, tn), lambda j, i: (layer, 0, j))],
        out_specs=pl.BlockSpec((tm, tn), lambda j, i: (i, j)),
        out_shape=jax.ShapeDtypeStruct((m, d), BF16),
        scratch_shapes=[pltpu.VMEM((d, tn), BF16), pltpu.VMEM((d, tn), BF16), pltpu.VMEM((d, tn), BF16),
                        pltpu.VMEM((cdim, tn), BF16)],
        compiler_params=_cparams(2),
        name="gate_merge",
    )(xn, attn, conv, w_in, w_in, w_attn_br, w_conv_br)


def _resid_norm_kernel(a_ref, w_ref, h_ref, g_ref, ho_ref, no_ref, wb_ref, *, rc):
    _cast_first(pl.program_id(0), w_ref, wb_ref)
    g = g_ref[...]
    for c in range(a_ref.shape[0] // rc):
        rows = pl.ds(c * rc, rc)
        h = h_ref[rows, :] + _dot(a_ref[rows, :], wb_ref[...])
        ho_ref[rows, :] = h
        no_ref[rows, :] = _rms(h, g).astype(no_ref.dtype)


def _resid_norm_call(a, w, layer, h, g, tm=512, rc=256):
    m, d = h.shape
    k = a.shape[1]
    row = lambda i: (i, 0)
    return pl.pallas_call(
        functools.partial(_resid_norm_kernel, rc=rc),
        grid=(m // tm,),
        in_specs=[pl.BlockSpec((tm, k), row),
                  pl.BlockSpec((None, k, d), lambda i: (layer, 0, 0), pipeline_mode=SINGLE),
                  pl.BlockSpec((tm, d), row),
                  pl.BlockSpec((1, d), lambda i: (0, 0))],
        out_specs=[pl.BlockSpec((tm, d), row), pl.BlockSpec((tm, d), row)],
        out_shape=[jax.ShapeDtypeStruct((m, d), F32), jax.ShapeDtypeStruct((m, d), BF16)],
        scratch_shapes=[pltpu.VMEM((k, d), BF16)],
        compiler_params=_cparams(1),
        name="mix_out_resid",
    )(a, w, h, g)


def _ffn_in_kernel(x_ref, wg_ref, wu_ref, cw_ref, cb_ref, o_ref, wgb_ref, wub_ref, pad_ref, *, rc):
    step = pl.program_id(1)
    _cast_first(step, wg_ref, wgb_ref)
    _cast_first(step, wu_ref, wub_ref)
    s = x_ref.shape[1]
    _zero_halo(pad_ref, s)
    for c in range(s // rc):
        pad_ref[pl.ds(CONV_PAD + c * rc, rc), :] = _dot(x_ref[0, pl.ds(c * rc, rc), :], wgb_ref[...])
    w = cw_ref[...]
    bias = cb_ref[...]
    for c in range(s // rc):
        rows = pl.ds(c * rc, rc)
        gate = _dwconv3_rows(pad_ref, c, rc, w) + bias
        up = _dot(x_ref[0, rows, :], wub_ref[...])
        o_ref[0, rows, :] = (gate * jax.nn.sigmoid(gate) * up).astype(o_ref.dtype)


def _ffn_in_call(hn, w_ffn_in, layer, conv_w, conv_b, *, d_ff, tn=512, rc=512):
    b, s, d = hn.shape
    nt = d_ff // tn
    return pl.pallas_call(
        functools.partial(_ffn_in_kernel, rc=rc),
        grid=(nt, b),
        in_specs=[pl.BlockSpec((1, s, d), lambda j, bi: (bi, 0, 0)),
                  pl.BlockSpec((None, d, tn), lambda j, bi: (layer, 0, j)),
                  pl.BlockSpec((None, d, tn), lambda j, bi: (layer, 0, nt + j)),
                  pl.BlockSpec((3, tn), lambda j, bi: (0, j)),
                  pl.BlockSpec((1, tn), lambda j, bi: (0, j))],
        out_specs=pl.BlockSpec((1, s, tn), lambda j, bi: (bi, 0, j)),
        out_shape=jax.ShapeDtypeStruct((b, s, d_ff), BF16),
        scratch_shapes=[pltpu.VMEM((d, tn), BF16), pltpu.VMEM((d, tn), BF16),
                        pltpu.VMEM((s + 2 * CONV_PAD, tn), F32)],
        compiler_params=_cparams(2),
        name="ffn_in",
    )(hn, w_ffn_in, w_ffn_in, conv_w, conv_b)


def _ffn_out_kernel(a_ref, w_ref, h_ref, ho_ref, wb_ref):
    _cast_first(pl.program_id(1), w_ref, wb_ref)
    ho_ref[...] = h_ref[...] + _dot(a_ref[...], wb_ref[...])


def _ffn_out_call(act, w, layer, h, tm=512, tn=1024):
    m, d = h.shape
    k = act.shape[1]
    return pl.pallas_call(
        _ffn_out_kernel,
        grid=(d // tn, m // tm),
        in_specs=[pl.BlockSpec((tm, k), lambda j, i: (i, 0)),
                  pl.BlockSpec((None, k, tn), lambda j, i: (layer, 0, j), pipeline_mode=SINGLE),
                  pl.BlockSpec((tm, tn), lambda j, i: (i, j))],
        out_specs=pl.BlockSpec((tm, tn), lambda j, i: (i, j)),
        out_shape=jax.ShapeDtypeStruct((m, d), F32),
        scratch_shapes=[pltpu.VMEM((k, tn), BF16)],
        compiler_params=_cparams(2),
        name="ffn_out_resid",
    )(act, w, h)


def _ple_kernel(h_ref, gp_ref, wg_ref, p_ref, wp_ref, gn_ref, *refs, last, rc):
    out_refs, (wgb_ref, wpb_ref) = refs[:-2], refs[-2:]
    _cast_first(pl.program_id(0), wg_ref, wgb_ref)
    _cast_first(pl.program_id(0), wp_ref, wpb_ref)
    gp = gp_ref[...]
    gn = gn_ref[...]
    for c in range(h_ref.shape[0] // rc):
        rows = pl.ds(c * rc, rc)
        h = h_ref[rows, :]
        gate = jax.nn.sigmoid(_dot(_rms(h, gp).astype(BF16), wgb_ref[...]))
        emb = _dot(p_ref[rows, :].astype(BF16), wpb_ref[...])
        h = h + gate * emb
        if last:
            out_refs[0][rows, :] = _rms(h, gn)
        else:
            out_refs[0][rows, :] = h
            out_refs[1][rows, :] = _rms(h, gn).astype(out_refs[1].dtype)


def _ple_call(h, g_ple, w_gate, p, layer, w_ple, g_next, *, last, tm=512, rc=128):
    m, d = h.shape
    pd = p.shape[-1]
    row = lambda i: (i, 0)
    const = lambda i: (0, 0)
    if last:
        out_specs = [pl.BlockSpec((tm, d), row)]
        out_shape = [jax.ShapeDtypeStruct((m, d), F32)]
    else:
        out_specs = [pl.BlockSpec((tm, d), row), pl.BlockSpec((tm, d), row)]
        out_shape = [jax.ShapeDtypeStruct((m, d), F32), jax.ShapeDtypeStruct((m, d), BF16)]
    return pl.pallas_call(
        functools.partial(_ple_kernel, last=last, rc=rc),
        grid=(m // tm,),
        in_specs=[pl.BlockSpec((tm, d), row),
                  pl.BlockSpec((1, d), const),
                  pl.BlockSpec((None, d, d), lambda i: (layer, 0, 0), pipeline_mode=SINGLE),
                  pl.BlockSpec((None, tm, pd), lambda i: (layer, i, 0)),
                  pl.BlockSpec((None, pd, d), lambda i: (layer, 0, 0), pipeline_mode=SINGLE),
                  pl.BlockSpec((1, d), const)],
        out_specs=out_specs,
        out_shape=out_shape,
        scratch_shapes=[pltpu.VMEM((d, d), BF16), pltpu.VMEM((pd, d), BF16)],
        compiler_params=_cparams(1),
        name="ple_final" if last else "ple",
    )(h, g_ple, w_gate, p, w_ple, g_next)


def _rope_tables(seq):
    axis_dim = HEAD_DIM // 2
    rows = seq // GRID_W
    row_ids = jnp.repeat(jnp.arange(rows, dtype=F32), GRID_W)
    col_ids = jnp.tile(jnp.arange(GRID_W, dtype=F32), rows)
    inv_freq = 1.0 / (ROPE_THETA ** (jnp.arange(0, axis_dim, 2, dtype=F32) / axis_dim))
    ang_r = row_ids[:, None] * inv_freq[None, :]
    ang_c = col_ids[:, None] * inv_freq[None, :]
    ang = jnp.concatenate([ang_r, ang_c, ang_r, ang_c], axis=-1)
    sign = jnp.where(jnp.arange(HEAD_DIM) < axis_dim, -1.0, 1.0).astype(F32)
    return jnp.cos(ang), jnp.sin(ang) * sign[None, :]


def kernel(x, p, norm_mix_g, w_in, q_norm_g, k_norm_g, w_attn_br, conv_mix_w, conv_mix_b, w_conv_br, w_mix_out, norm_ffn_g, w_ffn_in, conv_ffn_w, conv_ffn_b, w_ffn_out, norm_ple_g, w_ple_gate, w_ple, final_norm_g):
    b, s, d = x.shape
    depth = w_in.shape[0]
    m = b * s
    n_kv = d // HEAD_DIM // Q_PER_KV
    kv_dim = n_kv * HEAD_DIM
    conv_dim = conv_mix_w.shape[-1]
    d_ff = conv_ffn_w.shape[-1]
    conv_col0 = d + 2 * kv_dim
    gate_col0 = conv_col0 + 3 * conv_dim
    pd = p.shape[-1]

    cos, sin = _rope_tables(s)
    p2 = p.reshape(depth, m, pd)

    h = x.reshape(m, d)
    xn = _norm_call(h, norm_mix_g[0].reshape(1, d))
    out = None
    for i in range(depth):
        xn3 = xn.reshape(b, s, d)
        q = _q_call(xn3, w_in, i, cos, sin, q_norm_g[i].reshape(1, HEAD_DIM))
        k, v = _kv_call(xn3, w_in, i, cos, sin, k_norm_g[i].reshape(1, HEAD_DIM), k_col0=d, n_kv=n_kv)
        conv = _convbr_call(xn3, w_in, i, conv_mix_w[i], conv_mix_b[i].reshape(1, conv_dim),
                            col0=conv_col0, width=conv_dim)
        attn = _attn_call(q, k, v)
        merged = _merge_call(xn, attn.reshape(m, d), conv.reshape(m, conv_dim), w_in, i,
                             w_attn_br, w_conv_br, ga_col0=gate_col0)
        h, hn = _resid_norm_call(merged, w_mix_out, i, h, norm_ffn_g[i].reshape(1, d))
        act = _ffn_in_call(hn.reshape(b, s, d), w_ffn_in, i, conv_ffn_w[i],
                           conv_ffn_b[i].reshape(1, d_ff), d_ff=d_ff)
        h = _ffn_out_call(act.reshape(m, d_ff), w_ffn_out, i, h)
        last = i == depth - 1
        g_next = final_norm_g if last else norm_mix_g[i + 1]
        res = _ple_call(h, norm_ple_g[i].reshape(1, d), w_ple_gate, p2, i, w_ple, g_next.reshape(1, d),
                        last=last)
        if last:
            out = res[0]
        else:
            h, xn = res
    return out.reshape(b, s, d)
```

```python
import functools

import numpy as np
import jax
import jax.numpy as jnp
from jax import lax
from jax.experimental import pallas as pl
from jax.experimental.pallas import tpu as pltpu

HEAD_DIM = 128
Q_PER_KV = 4
ROPE_THETA = 10000.0
GRID_W = 64
EPS = 1e-6
LOG2E = 1.4426950408889634
CONV_PAD = 8
V7X_VMEM_LIMIT = 60 * 1024 * 1024

F32 = jnp.float32
BF16 = jnp.bfloat16
SINGLE = pl.Buffered(1)


def _steps(total, tile):
    assert total % tile == 0, (total, tile)
    return total // tile


def _cparams(n_axes):
    return pltpu.CompilerParams(dimension_semantics=("arbitrary",) * n_axes,
                                vmem_limit_bytes=V7X_VMEM_LIMIT)


def _rms(x, g):
    ms = jnp.mean(x * x, axis=-1, keepdims=True)
    return x * lax.rsqrt(ms + EPS) * g


def _dot(a, b):
    return jnp.dot(a, b, preferred_element_type=F32)


def _cast_first(step, w_ref, wb_ref):
    @pl.when(step == 0)
    def _():
        wb_ref[...] = w_ref[...].astype(wb_ref.dtype)


def _norm_kernel(x_ref, g_ref, o_ref):
    o_ref[...] = _rms(x_ref[...], g_ref[...]).astype(o_ref.dtype)


def _norm_call(x, g, tm=1024):
    m, d = x.shape
    return pl.pallas_call(
        _norm_kernel,
        grid=(_steps(m, tm),),
        in_specs=[pl.BlockSpec((tm, d), lambda i: (i, 0)),
                  pl.BlockSpec((1, d), lambda i: (0, 0))],
        out_specs=pl.BlockSpec((tm, d), lambda i: (i, 0)),
        out_shape=jax.ShapeDtypeStruct((m, d), BF16),
        compiler_params=_cparams(1),
        name="init_norm",
    )(x, g)


def _swap_mid(x):
    lane = lax.broadcasted_iota(jnp.int32, x.shape, x.ndim - 1)
    seg = lane // (HEAD_DIM // 4)
    up = pltpu.roll(x, 3 * HEAD_DIM // 4, x.ndim - 1)
    dn = pltpu.roll(x, HEAD_DIM // 4, x.ndim - 1)
    return jnp.where(seg == 1, up, jnp.where(seg == 2, dn, x))


def _cast_swapped_first(step, w_ref, wb_ref):
    @pl.when(step == 0)
    def _():
        for hh in range(w_ref.shape[1] // HEAD_DIM):
            cols = slice(hh * HEAD_DIM, (hh + 1) * HEAD_DIM)
            wb_ref[:, cols] = _swap_mid(w_ref[:, cols]).astype(wb_ref.dtype)


def _rope_gain_tables(g_ref, scale, cos, sgn_sin):
    g = _swap_mid(jnp.broadcast_to(g_ref[...], (8, HEAD_DIM)))[0:1] * scale
    return cos * g, sgn_sin * pltpu.roll(jnp.broadcast_to(g, (8, HEAD_DIM)), HEAD_DIM // 2, 1)[0:1]


def _head_mean_matrix():
    r = lax.broadcasted_iota(jnp.int32, (2 * HEAD_DIM, 2 * HEAD_DIM), 0) // HEAD_DIM
    c = lax.broadcasted_iota(jnp.int32, (2 * HEAD_DIM, 2 * HEAD_DIM), 1) // HEAD_DIM
    return jnp.where(r == c, 1.0 / HEAD_DIM, 0.0).astype(BF16)


def _norm_rope_heads(z, ca, sb, bd, o_ref, rows):
    z2 = (z * z).astype(BF16)
    for pair in range(z.shape[1] // (2 * HEAD_DIM)):
        rs = lax.rsqrt(_dot(z2[:, pair * 2 * HEAD_DIM:(pair + 1) * 2 * HEAD_DIM], bd) + EPS)
        for sub in range(2):
            hh = 2 * pair + sub
            x = z[:, hh * HEAD_DIM:(hh + 1) * HEAD_DIM]
            y = (x * ca + pltpu.roll(x, HEAD_DIM // 2, 1) * sb) * rs[:, sub * HEAD_DIM:(sub + 1) * HEAD_DIM]
            o_ref[0, hh, rows, :] = y.astype(o_ref.dtype)


def _q_kernel(x_ref, w_ref, cos_ref, sin_ref, g_ref, o_ref, wb_ref, *, rc):
    _cast_swapped_first(pl.program_id(1), w_ref, wb_ref)
    s = x_ref.shape[1]
    scale = LOG2E / float(np.sqrt(HEAD_DIM))
    bd = _head_mean_matrix()
    for c in range(_steps(s, rc)):
        rows = pl.ds(c * rc, rc)
        ca, sb = _rope_gain_tables(g_ref, scale, cos_ref[rows, :], sin_ref[rows, :])
        _norm_rope_heads(_dot(x_ref[0, rows, :], wb_ref[...]), ca, sb, bd, o_ref, rows)


def _q_call(xn, w_in, layer, cos, sin, qg, *, tn=1024, rc=256):
    b, s, d = xn.shape
    hpt = tn // HEAD_DIM
    const = lambda j, bi: (0, 0)
    return pl.pallas_call(
        functools.partial(_q_kernel, rc=rc),
        grid=(_steps(d, tn), b),
        in_specs=[pl.BlockSpec((1, s, d), lambda j, bi: (bi, 0, 0)),
                  pl.BlockSpec((None, d, tn), lambda j, bi: (layer, 0, j)),
                  pl.BlockSpec((s, HEAD_DIM), const),
                  pl.BlockSpec((s, HEAD_DIM), const),
                  pl.BlockSpec((1, HEAD_DIM), const)],
        out_specs=pl.BlockSpec((1, hpt, s, HEAD_DIM), lambda j, bi: (bi, j, 0, 0)),
        out_shape=jax.ShapeDtypeStruct((b, d // HEAD_DIM, s, HEAD_DIM), BF16),
        scratch_shapes=[pltpu.VMEM((d, tn), BF16)],
        compiler_params=_cparams(2),
        name="q_proj",
    )(xn, w_in, cos, sin, qg)


def _kv_kernel(x_ref, wk_ref, wv_ref, cos_ref, sin_ref, g_ref, k_ref, v_ref, wkb_ref, wvb_ref, *, rc):
    _cast_swapped_first(pl.program_id(0), wk_ref, wkb_ref)
    _cast_first(pl.program_id(0), wv_ref, wvb_ref)
    s = x_ref.shape[1]
    n_kv = k_ref.shape[1]
    bd = _head_mean_matrix()
    for c in range(_steps(s, rc)):
        rows = pl.ds(c * rc, rc)
        x = x_ref[0, rows, :]
        ca, sb = _rope_gain_tables(g_ref, 1.0, cos_ref[rows, :], sin_ref[rows, :])
        _norm_rope_heads(_dot(x, wkb_ref[...]), ca, sb, bd, k_ref, rows)
        zv = _dot(x, wvb_ref[...])
        for hh in range(n_kv):
            v_ref[0, hh, rows, :] = zv[:, hh * HEAD_DIM:(hh + 1) * HEAD_DIM].astype(v_ref.dtype)


def _kv_call(xn, w_in, layer, cos, sin, kg, *, k_col0, n_kv, rc=512):
    b, s, d = xn.shape
    kv = n_kv * HEAD_DIM
    kt = _steps(k_col0, kv)
    const = lambda bi: (0, 0)
    return pl.pallas_call(
        functools.partial(_kv_kernel, rc=rc),
        grid=(b,),
        in_specs=[pl.BlockSpec((1, s, d), lambda bi: (bi, 0, 0)),
                  pl.BlockSpec((None, d, kv), lambda bi: (layer, 0, kt), pipeline_mode=SINGLE),
                  pl.BlockSpec((None, d, kv), lambda bi: (layer, 0, kt + 1), pipeline_mode=SINGLE),
                  pl.BlockSpec((s, HEAD_DIM), const),
                  pl.BlockSpec((s, HEAD_DIM), const),
                  pl.BlockSpec((1, HEAD_DIM), const)],
        out_specs=[pl.BlockSpec((1, n_kv, s, HEAD_DIM), lambda bi: (bi, 0, 0, 0)),
                   pl.BlockSpec((1, n_kv, s, HEAD_DIM), lambda bi: (bi, 0, 0, 0))],
        out_shape=[jax.ShapeDtypeStruct((b, n_kv, s, HEAD_DIM), BF16),
                   jax.ShapeDtypeStruct((b, n_kv, s, HEAD_DIM), BF16)],
        scratch_shapes=[pltpu.VMEM((d, kv), BF16), pltpu.VMEM((d, kv), BF16)],
        compiler_params=_cparams(1),
        name="kv_proj",
    )(xn, w_in, w_in, cos, sin, kg)


def _dwconv3_rows(pad_ref, c, rc, w):
    base = CONV_PAD + c * rc
    prev = pad_ref[pl.ds(base - 1, rc), :]
    cur = pad_ref[pl.ds(base, rc), :]
    nxt = pad_ref[pl.ds(base + 1, rc), :]
    return prev * w[0:1, :] + cur * w[1:2, :] + nxt * w[2:3, :]


def _zero_halo(pad_ref, s):
    zeros = jnp.zeros((CONV_PAD, pad_ref.shape[1]), pad_ref.dtype)
    pad_ref[pl.ds(0, CONV_PAD), :] = zeros
    pad_ref[pl.ds(CONV_PAD + s, CONV_PAD), :] = zeros


def _convbr_kernel(x_ref, wb_ref, wc_ref, wu_ref, cw_ref, cb_ref, o_ref, wbb_ref, wcb_ref, wub_ref, pad_ref,
                   *, rc):
    step = pl.program_id(1)
    _cast_first(step, wb_ref, wbb_ref)
    _cast_first(step, wc_ref, wcb_ref)
    _cast_first(step, wu_ref, wub_ref)
    s = x_ref.shape[1]
    _zero_halo(pad_ref, s)
    for c in range(_steps(s, rc)):
        x = x_ref[0, pl.ds(c * rc, rc), :]
        pad_ref[pl.ds(CONV_PAD + c * rc, rc), :] = _dot(x, wcb_ref[...]) * _dot(x, wub_ref[...])
    w = cw_ref[...]
    bias = cb_ref[...]
    for c in range(_steps(s, rc)):
        rows = pl.ds(c * rc, rc)
        conv = _dwconv3_rows(pad_ref, c, rc, w) + bias
        bg = _dot(x_ref[0, rows, :], wbb_ref[...])
        o_ref[0, rows, :] = (bg * conv).astype(o_ref.dtype)


def _convbr_call(xn, w_in, layer, conv_w, conv_b, *, col0, width, tn=256, rc=512):
    b, s, d = xn.shape
    t0 = _steps(col0, tn)
    nt = _steps(width, tn)
    wspec = lambda off: pl.BlockSpec((None, d, tn), lambda j, bi: (layer, 0, t0 + off * nt + j))
    return pl.pallas_call(
        functools.partial(_convbr_kernel, rc=rc),
        grid=(nt, b),
        in_specs=[pl.BlockSpec((1, s, d), lambda j, bi: (bi, 0, 0)),
                  wspec(0), wspec(1), wspec(2),
                  pl.BlockSpec((3, tn), lambda j, bi: (0, j)),
                  pl.BlockSpec((1, tn), lambda j, bi: (0, j))],
        out_specs=pl.BlockSpec((1, s, tn), lambda j, bi: (bi, 0, j)),
        out_shape=jax.ShapeDtypeStruct((b, s, width), BF16),
        scratch_shapes=[pltpu.VMEM((d, tn), BF16), pltpu.VMEM((d, tn), BF16), pltpu.VMEM((d, tn), BF16),
                        pltpu.VMEM((s + 2 * CONV_PAD, tn), F32)],
        compiler_params=_cparams(2),
        name="conv_branch",
    )(xn, w_in, w_in, w_in, conv_w, conv_b)


def _attn_kernel(q_ref, k_ref, v_ref, o_ref, ve_ref, *, rc):
    s = k_ref.shape[2]
    tq = q_ref.shape[2]

    def build_ve():
        ve_ref[:, :HEAD_DIM] = v_ref[0, 0]
        ve_ref[:, HEAD_DIM:] = jnp.ones((s, HEAD_DIM), ve_ref.dtype)

    if tq == s:
        build_ve()
    else:
        pl.when(pl.program_id(2) == 0)(build_ve)

    k = k_ref[0, 0]
    ve = ve_ref[...]
    for g in range(Q_PER_KV):
        for r in range(_steps(tq, rc)):
            rows = pl.ds(r * rc, rc)
            sc = lax.dot_general(q_ref[0, g, rows, :], k, (((1,), (1,)), ((), ())),
                                 preferred_element_type=F32)
            m = jnp.max(sc, axis=-1, keepdims=True)
            p = jnp.exp2(sc - m).astype(BF16)
            oe = _dot(p, ve)
            o = oe[:, :HEAD_DIM] / oe[:, HEAD_DIM:]
            o_ref[0, rows, g * HEAD_DIM:(g + 1) * HEAD_DIM] = o.astype(o_ref.dtype)


def _attn_call(q, k, v, *, tq=2048, rc=256):
    b, n_q, s, _ = q.shape
    n_kv = k.shape[1]
    gw = Q_PER_KV * HEAD_DIM
    return pl.pallas_call(
        functools.partial(_attn_kernel, rc=rc),
        grid=(b, n_kv, _steps(s, tq)),
        in_specs=[pl.BlockSpec((1, Q_PER_KV, tq, HEAD_DIM), lambda bi, h, i: (bi, h, i, 0)),
                  pl.BlockSpec((1, 1, s, HEAD_DIM), lambda bi, h, i: (bi, h, 0, 0)),
                  pl.BlockSpec((1, 1, s, HEAD_DIM), lambda bi, h, i: (bi, h, 0, 0))],
        out_specs=pl.BlockSpec((1, tq, gw), lambda bi, h, i: (bi, i, h)),
        out_shape=jax.ShapeDtypeStruct((b, s, n_q * HEAD_DIM), BF16),
        scratch_shapes=[pltpu.VMEM((s, 2 * HEAD_DIM), BF16)],
        compiler_params=_cparams(3),
        name="gqa_attention",
    )(q, k, v)


def _merge_kernel(x_ref, a_ref, c_ref, wga_ref, wgc_ref, wa_ref, wc_ref, o_ref,
                  wgab_ref, wgcb_ref, wab_ref, wcb_ref):
    step = pl.program_id(1)
    _cast_first(step, wga_ref, wgab_ref)
    _cast_first(step, wgc_ref, wgcb_ref)
    _cast_first(step, wa_ref, wab_ref)
    _cast_first(step, wc_ref, wcb_ref)
    x = x_ref[...]
    ga = jax.nn.sigmoid(_dot(x, wgab_ref[...]))
    gc = jax.nn.sigmoid(_dot(x, wgcb_ref[...]))
    merged = ga * _dot(a_ref[...], wab_ref[...]) + gc * _dot(c_ref[...], wcb_ref[...])
    o_ref[...] = merged.astype(o_ref.dtype)


def _merge_call(xn, attn, conv, w_in, layer, w_attn_br, w_conv_br, *, ga_col0, tm=512, tn=512):
    m, d = xn.shape
    cdim = conv.shape[1]
    ga0 = _steps(ga_col0, tn)
    gc0 = ga0 + _steps(d, tn)
    return pl.pallas_call(
        _merge_kernel,
        grid=(_steps(d, tn), _steps(m, tm)),
        in_specs=[pl.BlockSpec((tm, d), lambda j, i: (i, 0)),
                  pl.BlockSpec((tm, d), lambda j, i: (i, 0)),
                  pl.BlockSpec((tm, cdim), lambda j, i: (i, 0)),
                  pl.BlockSpec((None, d, tn), lambda j, i: (layer, 0, ga0 + j)),
                  pl.BlockSpec((None, d, tn), lambda j, i: (layer, 0, gc0 + j)),
                  pl.BlockSpec((None, d, tn), lambda j, i: (layer, 0, j)),
                  pl.BlockSpec((None, cdim, tn), lambda j, i: (layer, 0, j))],
        out_specs=pl.BlockSpec((tm, tn), lambda j, i: (i, j)),
        out_shape=jax.ShapeDtypeStruct((m, d), BF16),
        scratch_shapes=[pltpu.VMEM((d, tn), BF16), pltpu.VMEM((d, tn), BF16), pltpu.VMEM((d, tn), BF16),
                        pltpu.VMEM((cdim, tn), BF16)],
        compiler_params=_cparams(2),
        name="gate_merge",
    )(xn, attn, conv, w_in, w_in, w_attn_br, w_conv_br)


def _resid_norm_kernel(a_ref, w_ref, h_ref, g_ref, ho_ref, no_ref, wb_ref, *, rc):
    _cast_first(pl.program_id(0), w_ref, wb_ref)
    g = g_ref[...]
    for c in range(_steps(a_ref.shape[0], rc)):
        rows = pl.ds(c * rc, rc)
        h = h_ref[rows, :] + _dot(a_ref[rows, :], wb_ref[...])
        ho_ref[rows, :] = h
        no_ref[rows, :] = _rms(h, g).astype(no_ref.dtype)


def _resid_norm_call(a, w, layer, h, g, tm=512, rc=256):
    m, d = h.shape
    k = a.shape[1]
    row = lambda i: (i, 0)
    return pl.pallas_call(
        functools.partial(_resid_norm_kernel, rc=rc),
        grid=(_steps(m, tm),),
        in_specs=[pl.BlockSpec((tm, k), row),
                  pl.BlockSpec((None, k, d), lambda i: (layer, 0, 0), pipeline_mode=SINGLE),
                  pl.BlockSpec((tm, d), row),
                  pl.BlockSpec((1, d), lambda i: (0, 0))],
        out_specs=[pl.BlockSpec((tm, d), row), pl.BlockSpec((tm, d), row)],
        out_shape=[jax.ShapeDtypeStruct((m, d), F32), jax.ShapeDtypeStruct((m, d), BF16)],
        scratch_shapes=[pltpu.VMEM((k, d), BF16)],
        compiler_params=_cparams(1),
        name="mix_out_resid",
    )(a, w, h, g)


def _ffn_in_kernel(x_ref, wg_ref, wu_ref, cw_ref, cb_ref, o_ref, wgb_ref, wub_ref, pad_ref, *, rc):
    step = pl.program_id(1)
    _cast_first(step, wg_ref, wgb_ref)
    _cast_first(step, wu_ref, wub_ref)
    s = x_ref.shape[1]
    _zero_halo(pad_ref, s)
    for c in range(_steps(s, rc)):
        pad_ref[pl.ds(CONV_PAD + c * rc, rc), :] = _dot(x_ref[0, pl.ds(c * rc, rc), :], wgb_ref[...])
    w = cw_ref[...]
    bias = cb_ref[...]
    for c in range(_steps(s, rc)):
        rows = pl.ds(c * rc, rc)
        gate = _dwconv3_rows(pad_ref, c, rc, w) + bias
        up = _dot(x_ref[0, rows, :], wub_ref[...])
        o_ref[0, rows, :] = (gate * jax.nn.sigmoid(gate) * up).astype(o_ref.dtype)


def _ffn_in_call(hn, w_ffn_in, layer, conv_w, conv_b, *, d_ff, tn=512, rc=512):
    b, s, d = hn.shape
    nt = _steps(d_ff, tn)
    return pl.pallas_call(
        functools.partial(_ffn_in_kernel, rc=rc),
        grid=(nt, b),
        in_specs=[pl.BlockSpec((1, s, d), lambda j, bi: (bi, 0, 0)),
                  pl.BlockSpec((None, d, tn), lambda j, bi: (layer, 0, j)),
                  pl.BlockSpec((None, d, tn), lambda j, bi: (layer, 0, nt + j)),
                  pl.BlockSpec((3, tn), lambda j, bi: (0, j)),
                  pl.BlockSpec((1, tn), lambda j, bi: (0, j))],
        out_specs=pl.BlockSpec((1, s, tn), lambda j, bi: (bi, 0, j)),
        out_shape=jax.ShapeDtypeStruct((b, s, d_ff), BF16),
        scratch_shapes=[pltpu.VMEM((d, tn), BF16), pltpu.VMEM((d, tn), BF16),
                        pltpu.VMEM((s + 2 * CONV_PAD, tn), F32)],
        compiler_params=_cparams(2),
        name="ffn_in",
    )(hn, w_ffn_in, w_ffn_in, conv_w, conv_b)


def _ffn_out_kernel(a_ref, w_ref, h_ref, ho_ref, wb_ref):
    _cast_first(pl.program_id(1), w_ref, wb_ref)
    ho_ref[...] = h_ref[...] + _dot(a_ref[...], wb_ref[...])


def _ffn_out_call(act, w, layer, h, tm=512, tn=1024):
    m, d = h.shape
    k = act.shape[1]
    return pl.pallas_call(
        _ffn_out_kernel,
        grid=(_steps(d, tn), _steps(m, tm)),
        in_specs=[pl.BlockSpec((tm, k), lambda j, i: (i, 0)),
                  pl.BlockSpec((None, k, tn), lambda j, i: (layer, 0, j), pipeline_mode=SINGLE),
                  pl.BlockSpec((tm, tn), lambda j, i: (i, j))],
        out_specs=pl.BlockSpec((tm, tn), lambda j, i: (i, j)),
        out_shape=jax.ShapeDtypeStruct((m, d), F32),
        scratch_shapes=[pltpu.VMEM((k, tn), BF16)],
        compiler_params=_cparams(2),
        name="ffn_out_resid",
    )(act, w, h)


def _ple_kernel(h_ref, gp_ref, wg_ref, p_ref, wp_ref, gn_ref, *refs, last, rc):
    out_refs, (wgb_ref, wpb_ref) = refs[:-2], refs[-2:]
    _cast_first(pl.program_id(0), wg_ref, wgb_ref)
    _cast_first(pl.program_id(0), wp_ref, wpb_ref)
    gp = gp_ref[...]
    gn = gn_ref[...]
    for c in range(_steps(h_ref.shape[0], rc)):
        rows = pl.ds(c * rc, rc)
        h = h_ref[rows, :]
        gate = jax.nn.sigmoid(_dot(_rms(h, gp).astype(BF16), wgb_ref[...]))
        emb = _dot(p_ref[rows, :].astype(BF16), wpb_ref[...])
        h = h + gate * emb
        if last:
            out_refs[0][rows, :] = _rms(h, gn)
        else:
            out_refs[0][rows, :] = h
            out_refs[1][rows, :] = _rms(h, gn).astype(out_refs[1].dtype)


def _ple_call(h, g_ple, w_gate, p, layer, w_ple, g_next, *, last, tm=512, rc=128):
    m, d = h.shape
    pd = p.shape[-1]
    row = lambda i: (i, 0)
    const = lambda i: (0, 0)
    if last:
        out_specs = [pl.BlockSpec((tm, d), row)]
        out_shape = [jax.ShapeDtypeStruct((m, d), F32)]
    else:
        out_specs = [pl.BlockSpec((tm, d), row), pl.BlockSpec((tm, d), row)]
        out_shape = [jax.ShapeDtypeStruct((m, d), F32), jax.ShapeDtypeStruct((m, d), BF16)]
    return pl.pallas_call(
        functools.partial(_ple_kernel, last=last, rc=rc),
        grid=(_steps(m, tm),),
        in_specs=[pl.BlockSpec((tm, d), row),
                  pl.BlockSpec((1, d), const),
                  pl.BlockSpec((None, d, d), lambda i: (layer, 0, 0), pipeline_mode=SINGLE),
                  pl.BlockSpec((None, tm, pd), lambda i: (layer, i, 0)),
                  pl.BlockSpec((None, pd, d), lambda i: (layer, 0, 0), pipeline_mode=SINGLE),
                  pl.BlockSpec((1, d), const)],
        out_specs=out_specs,
        out_shape=out_shape,
        scratch_shapes=[pltpu.VMEM((d, d), BF16), pltpu.VMEM((pd, d), BF16)],
        compiler_params=_cparams(1),
        name="ple_final" if last else "ple",
    )(h, g_ple, w_gate, p, w_ple, g_next)


def _rope_tables(seq):
    axis_dim = HEAD_DIM // 2
    rows = seq // GRID_W
    row_ids = jnp.repeat(jnp.arange(rows, dtype=F32), GRID_W)
    col_ids = jnp.tile(jnp.arange(GRID_W, dtype=F32), rows)
    inv_freq = 1.0 / (ROPE_THETA ** (jnp.arange(0, axis_dim, 2, dtype=F32) / axis_dim))
    ang_r = row_ids[:, None] * inv_freq[None, :]
    ang_c = col_ids[:, None] * inv_freq[None, :]
    ang = jnp.concatenate([ang_r, ang_c, ang_r, ang_c], axis=-1)
    sign = jnp.where(jnp.arange(HEAD_DIM) < axis_dim, -1.0, 1.0).astype(F32)
    return jnp.cos(ang), jnp.sin(ang) * sign[None, :]


def kernel(x, p, norm_mix_g, w_in, q_norm_g, k_norm_g, w_attn_br, conv_mix_w, conv_mix_b, w_conv_br, w_mix_out, norm_ffn_g, w_ffn_in, conv_ffn_w, conv_ffn_b, w_ffn_out, norm_ple_g, w_ple_gate, w_ple, final_norm_g):
    b, s, d = x.shape
    depth = w_in.shape[0]
    m = b * s
    n_kv = d // HEAD_DIM // Q_PER_KV
    kv_dim = n_kv * HEAD_DIM
    conv_dim = conv_mix_w.shape[-1]
    d_ff = conv_ffn_w.shape[-1]
    conv_col0 = d + 2 * kv_dim
    gate_col0 = conv_col0 + 3 * conv_dim
    pd = p.shape[-1]
    assert x.dtype == F32 and w_in.dtype == F32
    assert w_in.shape[1:] == (d, gate_col0 + 2 * d) and w_ffn_in.shape[1:] == (d, 2 * d_ff)
    assert d % (Q_PER_KV * HEAD_DIM) == 0 and s % GRID_W == 0

    cos, sin = _rope_tables(s)
    p2 = p.reshape(depth, m, pd)

    h = x.reshape(m, d)
    xn = _norm_call(h, norm_mix_g[0].reshape(1, d))
    out = None
    for i in range(depth):
        xn3 = xn.reshape(b, s, d)
        q = _q_call(xn3, w_in, i, cos, sin, q_norm_g[i].reshape(1, HEAD_DIM))
        k, v = _kv_call(xn3, w_in, i, cos, sin, k_norm_g[i].reshape(1, HEAD_DIM), k_col0=d, n_kv=n_kv)
        conv = _convbr_call(xn3, w_in, i, conv_mix_w[i], conv_mix_b[i].reshape(1, conv_dim),
                            col0=conv_col0, width=conv_dim)
        attn = _attn_call(q, k, v)
        merged = _merge_call(xn, attn.reshape(m, d), conv.reshape(m, conv_dim), w_in, i,
                             w_attn_br, w_conv_br, ga_col0=gate_col0)
        h, hn = _resid_norm_call(merged, w_mix_out, i, h, norm_ffn_g[i].reshape(1, d))
        act = _ffn_in_call(hn.reshape(b, s, d), w_ffn_in, i, conv_ffn_w[i],
                           conv_ffn_b[i].reshape(1, d_ff), d_ff=d_ff)
        h = _ffn_out_call(act.reshape(m, d_ff), w_ffn_out, i, h)
        last = i == depth - 1
        g_next = final_norm_g if last else norm_mix_g[i + 1]
        res = _ple_call(h, norm_ple_g[i].reshape(1, d), w_ple_gate, p2, i, w_ple, g_next.reshape(1, d),
                        last=last)
        if last:
            out = res[0]
        else:
            h, xn = res
    return out.reshape(b, s, d)
```

```python
import functools

import numpy as np
import jax
import jax.numpy as jnp
from jax import lax
from jax.experimental import pallas as pl
from jax.experimental.pallas import tpu as pltpu

HEAD_DIM = 128
Q_PER_KV = 4
ROPE_THETA = 10000.0
GRID_W = 64
EPS = 1e-6
LOG2E = 1.4426950408889634
CONV_PAD = 8
V7X_VMEM_LIMIT = 60 * 1024 * 1024

F32 = jnp.float32
BF16 = jnp.bfloat16
SINGLE = pl.Buffered(1)


def _steps(total, tile):
    assert total % tile == 0, (total, tile)
    return total // tile


def _cparams(n_axes):
    return pltpu.CompilerParams(dimension_semantics=("arbitrary",) * n_axes,
                                vmem_limit_bytes=V7X_VMEM_LIMIT)


def _rms(x, g):
    ms = jnp.mean(x * x, axis=-1, keepdims=True)
    return x * lax.rsqrt(ms + EPS) * g


def _dot(a, b):
    return jnp.dot(a, b, preferred_element_type=F32)


def _cast_first(step, w_ref, wb_ref):
    @pl.when(step == 0)
    def _():
        wb_ref[...] = w_ref[...].astype(wb_ref.dtype)


def _swap_mid(x):
    lane = lax.broadcasted_iota(jnp.int32, x.shape, x.ndim - 1)
    seg = lane // (HEAD_DIM // 4)
    up = pltpu.roll(x, 3 * HEAD_DIM // 4, x.ndim - 1)
    dn = pltpu.roll(x, HEAD_DIM // 4, x.ndim - 1)
    return jnp.where(seg == 1, up, jnp.where(seg == 2, dn, x))


def _cast_swapped_first(step, w_ref, wb_ref):
    @pl.when(step == 0)
    def _():
        for hh in range(w_ref.shape[1] // HEAD_DIM):
            cols = slice(hh * HEAD_DIM, (hh + 1) * HEAD_DIM)
            wb_ref[:, cols] = _swap_mid(w_ref[:, cols]).astype(wb_ref.dtype)


def _rope_gain_tables(g_ref, scale, cos, sgn_sin):
    g = _swap_mid(jnp.broadcast_to(g_ref[...], (8, HEAD_DIM)))[0:1] * scale
    return cos * g, sgn_sin * pltpu.roll(jnp.broadcast_to(g, (8, HEAD_DIM)), HEAD_DIM // 2, 1)[0:1]


def _head_mean_matrix():
    r = lax.broadcasted_iota(jnp.int32, (2 * HEAD_DIM, 2 * HEAD_DIM), 0) // HEAD_DIM
    c = lax.broadcasted_iota(jnp.int32, (2 * HEAD_DIM, 2 * HEAD_DIM), 1) // HEAD_DIM
    return jnp.where(r == c, 1.0 / HEAD_DIM, 0.0).astype(BF16)


def _norm_rope_heads(z, ca, sb, bd, o_ref, rows):
    z2 = (z * z).astype(BF16)
    for pair in range(z.shape[1] // (2 * HEAD_DIM)):
        rs = lax.rsqrt(_dot(z2[:, pair * 2 * HEAD_DIM:(pair + 1) * 2 * HEAD_DIM], bd) + EPS)
        for sub in range(2):
            hh = 2 * pair + sub
            x = z[:, hh * HEAD_DIM:(hh + 1) * HEAD_DIM]
            y = (x * ca + pltpu.roll(x, HEAD_DIM // 2, 1) * sb) * rs[:, sub * HEAD_DIM:(sub + 1) * HEAD_DIM]
            o_ref[0, hh, rows, :] = y.astype(o_ref.dtype)


def _q_kernel(x_ref, w_ref, cos_ref, sin_ref, g_ref, o_ref, wb_ref, *, rc):
    _cast_swapped_first(pl.program_id(1), w_ref, wb_ref)
    s = x_ref.shape[1]
    scale = LOG2E / float(np.sqrt(HEAD_DIM))
    bd = _head_mean_matrix()
    for c in range(_steps(s, rc)):
        rows = pl.ds(c * rc, rc)
        ca, sb = _rope_gain_tables(g_ref, scale, cos_ref[rows, :], sin_ref[rows, :])
        _norm_rope_heads(_dot(x_ref[0, rows, :], wb_ref[...]), ca, sb, bd, o_ref, rows)


def _q_call(xn, w_in, layer, cos, sin, qg, *, tn=1024, rc=256):
    b, s, d = xn.shape
    hpt = tn // HEAD_DIM
    const = lambda j, bi: (0, 0)
    return pl.pallas_call(
        functools.partial(_q_kernel, rc=rc),
        grid=(_steps(d, tn), b),
        in_specs=[pl.BlockSpec((1, s, d), lambda j, bi: (bi, 0, 0)),
                  pl.BlockSpec((None, d, tn), lambda j, bi: (layer, 0, j)),
                  pl.BlockSpec((s, HEAD_DIM), const),
                  pl.BlockSpec((s, HEAD_DIM), const),
                  pl.BlockSpec((1, HEAD_DIM), const)],
        out_specs=pl.BlockSpec((1, hpt, s, HEAD_DIM), lambda j, bi: (bi, j, 0, 0)),
        out_shape=jax.ShapeDtypeStruct((b, d // HEAD_DIM, s, HEAD_DIM), BF16),
        scratch_shapes=[pltpu.VMEM((d, tn), BF16)],
        compiler_params=_cparams(2),
        name="q_proj",
    )(xn, w_in, cos, sin, qg)


def _kv_kernel(*refs, rc, norm_input):
    if norm_input:
        x_ref, gin_ref, wk_ref, wv_ref, cos_ref, sin_ref, g_ref, k_ref, v_ref, xn_ref, wkb_ref, wvb_ref = refs
    else:
        x_ref, wk_ref, wv_ref, cos_ref, sin_ref, g_ref, k_ref, v_ref, wkb_ref, wvb_ref = refs
    _cast_swapped_first(pl.program_id(0), wk_ref, wkb_ref)
    _cast_first(pl.program_id(0), wv_ref, wvb_ref)
    n_kv = k_ref.shape[1]
    bd = _head_mean_matrix()
    for c in range(_steps(x_ref.shape[0], rc)):
        rows = pl.ds(c * rc, rc)
        if norm_input:
            x = _rms(x_ref[rows, :], gin_ref[...]).astype(BF16)
            xn_ref[rows, :] = x
        else:
            x = x_ref[rows, :]
        ca, sb = _rope_gain_tables(g_ref, 1.0, cos_ref[rows, :], sin_ref[rows, :])
        _norm_rope_heads(_dot(x, wkb_ref[...]), ca, sb, bd, k_ref, rows)
        zv = _dot(x, wvb_ref[...])
        for hh in range(n_kv):
            v_ref[0, hh, rows, :] = zv[:, hh * HEAD_DIM:(hh + 1) * HEAD_DIM].astype(v_ref.dtype)


def _kv_call(x, w_in, layer, cos, sin, kg, *, seq, k_col0, n_kv, norm_g=None, tm=1024, rc=512):
    m, d = x.shape
    nb = _steps(seq, tm)
    kv = n_kv * HEAD_DIM
    kt = _steps(k_col0, kv)
    norm_input = norm_g is not None
    row = lambda i: (i, 0)
    const = lambda i: (0, 0)
    tab = lambda i: (i % nb, 0)
    head_blk = lambda i: (i // nb, 0, i % nb, 0)
    in_specs = [pl.BlockSpec((tm, d), row)]
    args = [x]
    if norm_input:
        in_specs.append(pl.BlockSpec((1, d), const))
        args.append(norm_g)
    in_specs += [pl.BlockSpec((None, d, kv), lambda i: (layer, 0, kt), pipeline_mode=SINGLE),
                 pl.BlockSpec((None, d, kv), lambda i: (layer, 0, kt + 1), pipeline_mode=SINGLE),
                 pl.BlockSpec((tm, HEAD_DIM), tab),
                 pl.BlockSpec((tm, HEAD_DIM), tab),
                 pl.BlockSpec((1, HEAD_DIM), const)]
    args += [w_in, w_in, cos, sin, kg]
    out_specs = [pl.BlockSpec((1, n_kv, tm, HEAD_DIM), head_blk), pl.BlockSpec((1, n_kv, tm, HEAD_DIM), head_blk)]
    out_shape = [jax.ShapeDtypeStruct((m // seq, n_kv, seq, HEAD_DIM), BF16)] * 2
    if norm_input:
        out_specs.append(pl.BlockSpec((tm, d), row))
        out_shape.append(jax.ShapeDtypeStruct((m, d), BF16))
    return pl.pallas_call(
        functools.partial(_kv_kernel, rc=rc, norm_input=norm_input),
        grid=(_steps(m, tm),),
        in_specs=in_specs,
        out_specs=out_specs,
        out_shape=out_shape,
        scratch_shapes=[pltpu.VMEM((d, kv), BF16), pltpu.VMEM((d, kv), BF16)],
        compiler_params=_cparams(1),
        name="kv_proj_norm" if norm_input else "kv_proj",
    )(*args)


def _dwconv3_rows(pad_ref, c, rc, w):
    base = CONV_PAD + c * rc
    prev = pad_ref[pl.ds(base - 1, rc), :]
    cur = pad_ref[pl.ds(base, rc), :]
    nxt = pad_ref[pl.ds(base + 1, rc), :]
    return prev * w[0:1, :] + cur * w[1:2, :] + nxt * w[2:3, :]


def _zero_halo(pad_ref, s):
    zeros = jnp.zeros((CONV_PAD, pad_ref.shape[1]), pad_ref.dtype)
    pad_ref[pl.ds(0, CONV_PAD), :] = zeros
    pad_ref[pl.ds(CONV_PAD + s, CONV_PAD), :] = zeros


def _convbr_kernel(x_ref, wb_ref, wc_ref, wu_ref, cw_ref, cb_ref, o_ref, wbb_ref, wcb_ref, wub_ref, pad_ref,
                   *, rc):
    step = pl.program_id(1)
    _cast_first(step, wb_ref, wbb_ref)
    _cast_first(step, wc_ref, wcb_ref)
    _cast_first(step, wu_ref, wub_ref)
    s = x_ref.shape[1]
    _zero_halo(pad_ref, s)
    for c in range(_steps(s, rc)):
        x = x_ref[0, pl.ds(c * rc, rc), :]
        pad_ref[pl.ds(CONV_PAD + c * rc, rc), :] = _dot(x, wcb_ref[...]) * _dot(x, wub_ref[...])
    w = cw_ref[...]
    bias = cb_ref[...]
    for c in range(_steps(s, rc)):
        rows = pl.ds(c * rc, rc)
        conv = _dwconv3_rows(pad_ref, c, rc, w) + bias
        bg = _dot(x_ref[0, rows, :], wbb_ref[...])
        o_ref[0, rows, :] = (bg * conv).astype(o_ref.dtype)


def _convbr_call(xn, w_in, layer, conv_w, conv_b, *, col0, width, tn=256, rc=512):
    b, s, d = xn.shape
    t0 = _steps(col0, tn)
    nt = _steps(width, tn)
    wspec = lambda off: pl.BlockSpec((None, d, tn), lambda j, bi: (layer, 0, t0 + off * nt + j))
    return pl.pallas_call(
        functools.partial(_convbr_kernel, rc=rc),
        grid=(nt, b),
        in_specs=[pl.BlockSpec((1, s, d), lambda j, bi: (bi, 0, 0)),
                  wspec(0), wspec(1), wspec(2),
                  pl.BlockSpec((3, tn), lambda j, bi: (0, j)),
                  pl.BlockSpec((1, tn), lambda j, bi: (0, j))],
        out_specs=pl.BlockSpec((1, s, tn), lambda j, bi: (bi, 0, j)),
        out_shape=jax.ShapeDtypeStruct((b, s, width), BF16),
        scratch_shapes=[pltpu.VMEM((d, tn), BF16), pltpu.VMEM((d, tn), BF16), pltpu.VMEM((d, tn), BF16),
                        pltpu.VMEM((s + 2 * CONV_PAD, tn), F32)],
        compiler_params=_cparams(2),
        name="conv_branch",
    )(xn, w_in, w_in, w_in, conv_w, conv_b)


def _attn_kernel(q_ref, k_ref, v_ref, o_ref, ve_ref, *, rc):
    s = k_ref.shape[2]
    tq = q_ref.shape[2]

    def build_ve():
        ve_ref[:, :HEAD_DIM] = v_ref[0, 0]
        ve_ref[:, HEAD_DIM:] = jnp.ones((s, HEAD_DIM), ve_ref.dtype)

    if tq == s:
        build_ve()
    else:
        pl.when(pl.program_id(2) == 0)(build_ve)

    k = k_ref[0, 0]
    ve = ve_ref[...]
    for g in range(Q_PER_KV):
        for r in range(_steps(tq, rc)):
            rows = pl.ds(r * rc, rc)
            sc = lax.dot_general(q_ref[0, g, rows, :], k, (((1,), (1,)), ((), ())),
                                 preferred_element_type=F32)
            m = jnp.max(sc, axis=-1, keepdims=True)
            p = jnp.exp2(sc - m).astype(BF16)
            oe = _dot(p, ve)
            o = oe[:, :HEAD_DIM] / oe[:, HEAD_DIM:]
            o_ref[0, rows, g * HEAD_DIM:(g + 1) * HEAD_DIM] = o.astype(o_ref.dtype)


def _attn_call(q, k, v, *, tq=2048, rc=256):
    b, n_q, s, _ = q.shape
    n_kv = k.shape[1]
    gw = Q_PER_KV * HEAD_DIM
    return pl.pallas_call(
        functools.partial(_attn_kernel, rc=rc),
        grid=(b, n_kv, _steps(s, tq)),
        in_specs=[pl.BlockSpec((1, Q_PER_KV, tq, HEAD_DIM), lambda bi, h, i: (bi, h, i, 0)),
                  pl.BlockSpec((1, 1, s, HEAD_DIM), lambda bi, h, i: (bi, h, 0, 0)),
                  pl.BlockSpec((1, 1, s, HEAD_DIM), lambda bi, h, i: (bi, h, 0, 0))],
        out_specs=pl.BlockSpec((1, tq, gw), lambda bi, h, i: (bi, i, h)),
        out_shape=jax.ShapeDtypeStruct((b, s, n_q * HEAD_DIM), BF16),
        scratch_shapes=[pltpu.VMEM((s, 2 * HEAD_DIM), BF16)],
        compiler_params=_cparams(3),
        name="gqa_attention",
    )(q, k, v)


def _merge_kernel(x_ref, a_ref, c_ref, wga_ref, wgc_ref, wa_ref, wc_ref, o_ref,
                  wgab_ref, wgcb_ref, wab_ref, wcb_ref):
    step = pl.program_id(1)
    _cast_first(step, wga_ref, wgab_ref)
    _cast_first(step, wgc_ref, wgcb_ref)
    _cast_first(step, wa_ref, wab_ref)
    _cast_first(step, wc_ref, wcb_ref)
    x = x_ref[...]
    ga = jax.nn.sigmoid(_dot(x, wgab_ref[...]))
    gc = jax.nn.sigmoid(_dot(x, wgcb_ref[...]))
    merged = ga * _dot(a_ref[...], wab_ref[...]) + gc * _dot(c_ref[...], wcb_ref[...])
    o_ref[...] = merged.astype(o_ref.dtype)


def _merge_call(xn, attn, conv, w_in, layer, w_attn_br, w_conv_br, *, ga_col0, tm=512, tn=512):
    m, d = xn.shape
    cdim = conv.shape[1]
    ga0 = _steps(ga_col0, tn)
    gc0 = ga0 + _steps(d, tn)
    return pl.pallas_call(
        _merge_kernel,
        grid=(_steps(d, tn), _steps(m, tm)),
        in_specs=[pl.BlockSpec((tm, d), lambda j, i: (i, 0)),
                  pl.BlockSpec((tm, d), lambda j, i: (i, 0)),
                  pl.BlockSpec((tm, cdim), lambda j, i: (i, 0)),
                  pl.BlockSpec((None, d, tn), lambda j, i: (layer, 0, ga0 + j)),
                  pl.BlockSpec((None, d, tn), lambda j, i: (layer, 0, gc0 + j)),
                  pl.BlockSpec((None, d, tn), lambda j, i: (layer, 0, j)),
                  pl.BlockSpec((None, cdim, tn), lambda j, i: (layer, 0, j))],
        out_specs=pl.BlockSpec((tm, tn), lambda j, i: (i, j)),
        out_shape=jax.ShapeDtypeStruct((m, d), BF16),
        scratch_shapes=[pltpu.VMEM((d, tn), BF16), pltpu.VMEM((d, tn), BF16), pltpu.VMEM((d, tn), BF16),
                        pltpu.VMEM((cdim, tn), BF16)],
        compiler_params=_cparams(2),
        name="gate_merge",
    )(xn, attn, conv, w_in, w_in, w_attn_br, w_conv_br)


def _resid_norm_kernel(a_ref, w_ref, h_ref, g_ref, ho_ref, no_ref, wb_ref, *, rc):
    _cast_first(pl.program_id(0), w_ref, wb_ref)
    g = g_ref[...]
    for c in range(_steps(a_ref.shape[0], rc)):
        rows = pl.ds(c * rc, rc)
        h = h_ref[rows, :] + _dot(a_ref[rows, :], wb_ref[...])
        ho_ref[rows, :] = h
        no_ref[rows, :] = _rms(h, g).astype(no_ref.dtype)


def _resid_norm_call(a, w, layer, h, g, tm=512, rc=256):
    m, d = h.shape
    k = a.shape[1]
    row = lambda i: (i, 0)
    return pl.pallas_call(
        functools.partial(_resid_norm_kernel, rc=rc),
        grid=(_steps(m, tm),),
        in_specs=[pl.BlockSpec((tm, k), row),
                  pl.BlockSpec((None, k, d), lambda i: (layer, 0, 0), pipeline_mode=SINGLE),
                  pl.BlockSpec((tm, d), row),
                  pl.BlockSpec((1, d), lambda i: (0, 0))],
        out_specs=[pl.BlockSpec((tm, d), row), pl.BlockSpec((tm, d), row)],
        out_shape=[jax.ShapeDtypeStruct((m, d), F32), jax.ShapeDtypeStruct((m, d), BF16)],
        scratch_shapes=[pltpu.VMEM((k, d), BF16)],
        compiler_params=_cparams(1),
        name="mix_out_resid",
    )(a, w, h, g)


def _ffn_in_kernel(x_ref, wg_ref, wu_ref, cw_ref, cb_ref, o_ref, wgb_ref, wub_ref, pad_ref, *, rc):
    step = pl.program_id(1)
    _cast_first(step, wg_ref, wgb_ref)
    _cast_first(step, wu_ref, wub_ref)
    s = x_ref.shape[1]
    _zero_halo(pad_ref, s)
    for c in range(_steps(s, rc)):
        pad_ref[pl.ds(CONV_PAD + c * rc, rc), :] = _dot(x_ref[0, pl.ds(c * rc, rc), :], wgb_ref[...])
    w = cw_ref[...]
    bias = cb_ref[...]
    for c in range(_steps(s, rc)):
        rows = pl.ds(c * rc, rc)
        gate = _dwconv3_rows(pad_ref, c, rc, w) + bias
        up = _dot(x_ref[0, rows, :], wub_ref[...])
        o_ref[0, rows, :] = (gate * jax.nn.sigmoid(gate) * up).astype(o_ref.dtype)


def _ffn_in_call(hn, w_ffn_in, layer, conv_w, conv_b, *, d_ff, tn=512, rc=512):
    b, s, d = hn.shape
    nt = _steps(d_ff, tn)
    return pl.pallas_call(
        functools.partial(_ffn_in_kernel, rc=rc),
        grid=(nt, b),
        in_specs=[pl.BlockSpec((1, s, d), lambda j, bi: (bi, 0, 0)),
                  pl.BlockSpec((None, d, tn), lambda j, bi: (layer, 0, j)),
                  pl.BlockSpec((None, d, tn), lambda j, bi: (layer, 0, nt + j)),
                  pl.BlockSpec((3, tn), lambda j, bi: (0, j)),
                  pl.BlockSpec((1, tn), lambda j, bi: (0, j))],
        out_specs=pl.BlockSpec((1, s, tn), lambda j, bi: (bi, 0, j)),
        out_shape=jax.ShapeDtypeStruct((b, s, d_ff), BF16),
        scratch_shapes=[pltpu.VMEM((d, tn), BF16), pltpu.VMEM((d, tn), BF16),
                        pltpu.VMEM((s + 2 * CONV_PAD, tn), F32)],
        compiler_params=_cparams(2),
        name="ffn_in",
    )(hn, w_ffn_in, w_ffn_in, conv_w, conv_b)


def _ffn_out_kernel(a_ref, w_ref, h_ref, ho_ref, wb_ref):
    _cast_first(pl.program_id(1), w_ref, wb_ref)
    ho_ref[...] = h_ref[...] + _dot(a_ref[...], wb_ref[...])


def _ffn_out_call(act, w, layer, h, tm=512, tn=1024):
    m, d = h.shape
    k = act.shape[1]
    return pl.pallas_call(
        _ffn_out_kernel,
        grid=(_steps(d, tn), _steps(m, tm)),
        in_specs=[pl.BlockSpec((tm, k), lambda j, i: (i, 0)),
                  pl.BlockSpec((None, k, tn), lambda j, i: (layer, 0, j), pipeline_mode=SINGLE),
                  pl.BlockSpec((tm, tn), lambda j, i: (i, j))],
        out_specs=pl.BlockSpec((tm, tn), lambda j, i: (i, j)),
        out_shape=jax.ShapeDtypeStruct((m, d), F32),
        scratch_shapes=[pltpu.VMEM((k, tn), BF16)],
        compiler_params=_cparams(2),
        name="ffn_out_resid",
    )(act, w, h)


def _ple_kernel(h_ref, gp_ref, wg_ref, p_ref, wp_ref, gn_ref, *refs, last, rc):
    out_refs, (wgb_ref, wpb_ref) = refs[:-2], refs[-2:]
    _cast_first(pl.program_id(0), wg_ref, wgb_ref)
    _cast_first(pl.program_id(0), wp_ref, wpb_ref)
    gp = gp_ref[...]
    gn = gn_ref[...]
    for c in range(_steps(h_ref.shape[0], rc)):
        rows = pl.ds(c * rc, rc)
        h = h_ref[rows, :]
        gate = jax.nn.sigmoid(_dot(_rms(h, gp).astype(BF16), wgb_ref[...]))
        emb = _dot(p_ref[rows, :].astype(BF16), wpb_ref[...])
        h = h + gate * emb
        if last:
            out_refs[0][rows, :] = _rms(h, gn)
        else:
            out_refs[0][rows, :] = h
            out_refs[1][rows, :] = _rms(h, gn).astype(out_refs[1].dtype)


def _ple_call(h, g_ple, w_gate, p, layer, w_ple, g_next, *, last, tm=512, rc=128):
    m, d = h.shape
    pd = p.shape[-1]
    row = lambda i: (i, 0)
    const = lambda i: (0, 0)
    if last:
        out_specs = [pl.BlockSpec((tm, d), row)]
        out_shape = [jax.ShapeDtypeStruct((m, d), F32)]
    else:
        out_specs = [pl.BlockSpec((tm, d), row), pl.BlockSpec((tm, d), row)]
        out_shape = [jax.ShapeDtypeStruct((m, d), F32), jax.ShapeDtypeStruct((m, d), BF16)]
    return pl.pallas_call(
        functools.partial(_ple_kernel, last=last, rc=rc),
        grid=(_steps(m, tm),),
        in_specs=[pl.BlockSpec((tm, d), row),
                  pl.BlockSpec((1, d), const),
                  pl.BlockSpec((None, d, d), lambda i: (layer, 0, 0), pipeline_mode=SINGLE),
                  pl.BlockSpec((None, tm, pd), lambda i: (layer, i, 0)),
                  pl.BlockSpec((None, pd, d), lambda i: (layer, 0, 0), pipeline_mode=SINGLE),
                  pl.BlockSpec((1, d), const)],
        out_specs=out_specs,
        out_shape=out_shape,
        scratch_shapes=[pltpu.VMEM((d, d), BF16), pltpu.VMEM((pd, d), BF16)],
        compiler_params=_cparams(1),
        name="ple_final" if last else "ple",
    )(h, g_ple, w_gate, p, w_ple, g_next)


def _rope_tables(seq):
    axis_dim = HEAD_DIM // 2
    rows = seq // GRID_W
    row_ids = jnp.repeat(jnp.arange(rows, dtype=F32), GRID_W)
    col_ids = jnp.tile(jnp.arange(GRID_W, dtype=F32), rows)
    inv_freq = 1.0 / (ROPE_THETA ** (jnp.arange(0, axis_dim, 2, dtype=F32) / axis_dim))
    ang_r = row_ids[:, None] * inv_freq[None, :]
    ang_c = col_ids[:, None] * inv_freq[None, :]
    ang = jnp.concatenate([ang_r, ang_c, ang_r, ang_c], axis=-1)
    sign = jnp.where(jnp.arange(HEAD_DIM) < axis_dim, -1.0, 1.0).astype(F32)
    return jnp.cos(ang), jnp.sin(ang) * sign[None, :]


def kernel(x, p, norm_mix_g, w_in, q_norm_g, k_norm_g, w_attn_br, conv_mix_w, conv_mix_b, w_conv_br, w_mix_out, norm_ffn_g, w_ffn_in, conv_ffn_w, conv_ffn_b, w_ffn_out, norm_ple_g, w_ple_gate, w_ple, final_norm_g):
    b, s, d = x.shape
    depth = w_in.shape[0]
    m = b * s
    n_kv = d // HEAD_DIM // Q_PER_KV
    kv_dim = n_kv * HEAD_DIM
    conv_dim = conv_mix_w.shape[-1]
    d_ff = conv_ffn_w.shape[-1]
    conv_col0 = d + 2 * kv_dim
    gate_col0 = conv_col0 + 3 * conv_dim
    pd = p.shape[-1]
    assert x.dtype == F32 and w_in.dtype == F32
    assert w_in.shape[1:] == (d, gate_col0 + 2 * d) and w_ffn_in.shape[1:] == (d, 2 * d_ff)
    assert d % (Q_PER_KV * HEAD_DIM) == 0 and s % GRID_W == 0

    cos, sin = _rope_tables(s)
    p2 = p.reshape(depth, m, pd)

    h = x.reshape(m, d)
    xn = None
    out = None
    for i in range(depth):
        kg = k_norm_g[i].reshape(1, HEAD_DIM)
        if i == 0:
            k, v, xn = _kv_call(h, w_in, i, cos, sin, kg, seq=s, k_col0=d, n_kv=n_kv,
                                norm_g=norm_mix_g[0].reshape(1, d))
        else:
            k, v = _kv_call(xn, w_in, i, cos, sin, kg, seq=s, k_col0=d, n_kv=n_kv)
        xn3 = xn.reshape(b, s, d)
        q = _q_call(xn3, w_in, i, cos, sin, q_norm_g[i].reshape(1, HEAD_DIM))
        conv = _convbr_call(xn3, w_in, i, conv_mix_w[i], conv_mix_b[i].reshape(1, conv_dim),
                            col0=conv_col0, width=conv_dim)
        attn = _attn_call(q, k, v)
        merged = _merge_call(xn, attn.reshape(m, d), conv.reshape(m, conv_dim), w_in, i,
                             w_attn_br, w_conv_br, ga_col0=gate_col0)
        h, hn = _resid_norm_call(merged, w_mix_out, i, h, norm_ffn_g[i].reshape(1, d))
        act = _ffn_in_call(hn.reshape(b, s, d), w_ffn_in, i, conv_ffn_w[i],
                           conv_ffn_b[i].reshape(1, d_ff), d_ff=d_ff)
        h = _ffn_out_call(act.reshape(m, d_ff), w_ffn_out, i, h)
        last = i == depth - 1
        g_next = final_norm_g if last else norm_mix_g[i + 1]
        res = _ple_call(h, norm_ple_g[i].reshape(1, d), w_ple_gate, p2, i, w_ple, g_next.reshape(1, d),
                        last=last)
        if last:
            out = res[0]
        else:
            h, xn = res
    return out.reshape(b, s, d)
```
